```python
import math, functools
import jax, jax.numpy as jnp
from jax import lax
import numpy as np

D_MODEL = 1024
BATCH = 2
SEQ = 16384
DEPTH = 1
DEC_BATCH = 128
DEC_SEQ = 8
PAST_LEN = 8192
PAGE_SIZE = 128

DA_HEADS = 8
DA_KV_HEADS = 4
DA_GROUP = DA_HEADS // DA_KV_HEADS
DA_HEAD_DIM = 64
FOX_HEADS = 16
FOX_KV_HEADS = 4
FOX_GROUP = FOX_HEADS // FOX_KV_HEADS
FOX_HEAD_DIM = 64
ROPE_THETA = 10000.0
Q_BLOCK = 128
N_EXPERTS = 256
TOP_K = 8
N_GROUPS = 8
TOPK_GROUPS = 4
D_EXPERT = 256
D_SHARED = 256
ROUTED_SCALE = 2.5
EXPERT_BLOCK = 64
D_PLE = 256
EPS = 1e-5
DEEPNORM_ALPHA = (2 * DEPTH) ** 0.25
DEEPNORM_BETA = (8 * DEPTH) ** -0.25
DA_Q_W = DA_HEADS * 2 * DA_HEAD_DIM
DA_KV_W = DA_KV_HEADS * 2 * DA_HEAD_DIM
FOX_Q_W = FOX_HEADS * FOX_HEAD_DIM
FOX_KV_W = FOX_KV_HEADS * FOX_HEAD_DIM
IN_SPLITS = (DA_Q_W, DA_KV_W, DA_KV_W, FOX_Q_W, FOX_KV_W, FOX_KV_W, FOX_HEADS, 2 * D_MODEL)
D_IN = sum(IN_SPLITS)
DA_OUT_W = DA_HEADS * 2 * DA_HEAD_DIM
FOX_OUT_W = FOX_HEADS * FOX_HEAD_DIM

kernel_name = 'diff_fox_gated_moe_decoder_step'


def _layer_norm(x, g, b):
    xf = x.astype(jnp.float32)
    mu = jnp.mean(xf, -1, keepdims=True)
    var = jnp.mean(jnp.square(xf - mu), -1, keepdims=True)
    return ((xf - mu) * lax.rsqrt(var + EPS) * g + b).astype(x.dtype)


def _rms_norm(x, g):
    xf = x.astype(jnp.float32)
    return xf * lax.rsqrt(jnp.mean(xf * xf, -1, keepdims=True) + EPS) * g


def _rope(x, pos):
    d = x.shape[-1]
    inv = ROPE_THETA ** (-jnp.arange(0, d, 2, dtype=jnp.float32) / d)
    ang = pos.astype(jnp.float32)[:, None] * inv[None, :]
    shp = (ang.shape[0],) + (1,) * (x.ndim - 3) + (d // 2,)
    cos, sin = jnp.cos(ang).reshape(shp), jnp.sin(ang).reshape(shp)
    xf = x.astype(jnp.float32)
    x1, x2 = xf[..., : d // 2], xf[..., d // 2:]
    return jnp.concatenate([x1 * cos - x2 * sin, x2 * cos + x1 * sin], -1).astype(x.dtype)


def _mixer_inputs(x, w_in, b_forget, pos):
    B, T, _ = x.shape
    cuts = [int(c) for c in np.cumsum(IN_SPLITS)[:-1]]
    q_da, k_da, v_da, q_fx, k_fx, v_fx, f_fx, gates = jnp.split(x @ w_in, cuts, axis=-1)
    q_da = _rope(q_da.reshape(B, T, DA_KV_HEADS, DA_GROUP, 2, DA_HEAD_DIM), pos)
    k_da = _rope(k_da.reshape(B, T, DA_KV_HEADS, 2, DA_HEAD_DIM), pos)
    v_da = v_da.reshape(B, T, DA_KV_HEADS, 2 * DA_HEAD_DIM)
    q_fx = q_fx.reshape(B, T, FOX_KV_HEADS, FOX_GROUP, FOX_HEAD_DIM)
    k_fx = k_fx.reshape(B, T, FOX_KV_HEADS, FOX_HEAD_DIM)
    v_fx = v_fx.reshape(B, T, FOX_KV_HEADS, FOX_HEAD_DIM)
    logf = jax.nn.log_sigmoid(f_fx.astype(jnp.float32) + b_forget.astype(jnp.float32))
    g_a = jax.nn.sigmoid(gates[..., :D_MODEL])
    g_b = jax.nn.sigmoid(gates[..., D_MODEL:])
    return q_da, k_da, v_da, q_fx, k_fx, v_fx, logf, g_a, g_b


def _diff_attn(q, segs, lam, subln_g, lam_init):
    B, Tq = q.shape[:2]
    scale = DA_HEAD_DIM ** -0.5
    scores, sizes = [], []
    for k, v, mask in segs:
        s = jnp.einsum('bqhgmd,bkhmd->bhgmqk', q, k, preferred_element_type=jnp.float32) * scale
        if mask is not None:
            s = jnp.where(mask, s, -jnp.inf)
        scores.append(s)
        sizes.append(k.shape[1])
    p = jax.nn.softmax(jnp.concatenate(scores, -1), axis=-1)
    a = p[:, :, :, 0] - lam * p[:, :, :, 1]
    parts = jnp.split(a, [int(c) for c in np.cumsum(sizes)[:-1]], axis=-1)
    o = jnp.einsum('bhgqk,bkhe->bqhge', parts[0], segs[0][1])
    for ai, (_, v, _) in zip(parts[1:], segs[1:]):
        o = o + jnp.einsum('bhgqk,bkhe->bqhge', ai, v)
    o = _rms_norm(o, subln_g) * (1.0 - lam_init)
    return o.reshape(B, Tq, DA_OUT_W).astype(q.dtype)


def _fox_attn(q, segs):
    B, Tq = q.shape[:2]
    scale = FOX_HEAD_DIM ** -0.5
    scores, sizes = [], []
    for k, v, bias in segs:
        scores.append(jnp.einsum('bqhgd,bkhd->bhgqk', q, k, preferred_element_type=jnp.float32) * scale + bias)
        sizes.append(k.shape[1])
    p = jax.nn.softmax(jnp.concatenate(scores, -1), axis=-1)
    parts = jnp.split(p, [int(c) for c in np.cumsum(sizes)[:-1]], axis=-1)
    o = jnp.einsum('bhgqk,bkhd->bqhgd', parts[0], segs[0][1])
    for pi, (_, v, _) in zip(parts[1:], segs[1:]):
        o = o + jnp.einsum('bhgqk,bkhd->bqhgd', pi, v)
    return o.reshape(B, Tq, FOX_OUT_W).astype(q.dtype)


def _head_major(c):
    B, T = c.shape[:2]
    return jnp.moveaxis(c.reshape(B, T, FOX_KV_HEADS, FOX_GROUP), 1, -1)


def _to_blocks(a):
    B, T = a.shape[:2]
    return jnp.moveaxis(a.reshape((B, T // Q_BLOCK, Q_BLOCK) + a.shape[2:]), 1, 0)


def _from_blocks(a):
    a = jnp.moveaxis(a, 0, 1)
    return a.reshape((a.shape[0], a.shape[1] * a.shape[2]) + a.shape[3:])


def _mixers_prompt(q_da, k_da, v_da, q_fx, k_fx, v_fx, logf, lam, subln_g, lam_init):
    T = q_da.shape[1]
    c = jnp.cumsum(logf, axis=1)
    c_keys = _head_major(c)
    key_pos = jnp.arange(T)

    def block(args):
        qa, qf, cq, start = args
        q_pos = start + jnp.arange(Q_BLOCK)
        mask = key_pos[None, :] <= q_pos[:, None]
        oa = _diff_attn(qa, [(k_da, v_da, mask)], lam, subln_g, lam_init)
        bias = jnp.where(mask, _head_major(cq)[..., :, None] - c_keys[..., None, :], -jnp.inf)
        of = _fox_attn(qf, [(k_fx, v_fx, bias)])
        return oa, of

    starts = jnp.arange(T // Q_BLOCK) * Q_BLOCK
    oa, of = lax.map(block, (_to_blocks(q_da), _to_blocks(q_fx), _to_blocks(c), starts))
    return _from_blocks(oa), _from_blocks(of)


def _mixers_sample(q_da, k_da, v_da, q_fx, k_fx, v_fx, logf, pk_da, pv_da, pk_fx, pv_fx, plogf, lam, subln_g, lam_init):
    Tq = q_da.shape[1]
    causal = jnp.tril(jnp.ones((Tq, Tq), dtype=bool))
    oa = _diff_attn(q_da, [(pk_da, pv_da, None), (k_da, v_da, causal)], lam, subln_g, lam_init)
    c_past = jnp.cumsum(plogf.astype(jnp.float32), axis=1)
    c_new = c_past[:, -1:] + jnp.cumsum(logf, axis=1)
    cq = _head_major(c_new)
    bias_past = cq[..., :, None] - _head_major(c_past)[..., None, :]
    bias_new = jnp.where(causal, cq[..., :, None] - cq[..., None, :], -jnp.inf)
    of = _fox_attn(q_fx, [(pk_fx, pv_fx, bias_past), (k_fx, v_fx, bias_new)])
    return oa, of


def _gather_pages(cache, l, page_table):
    g = cache[l, page_table]
    return g.reshape((g.shape[0], g.shape[1] * g.shape[2]) + g.shape[3:])


def _route(x2d, w_router, router_bias):
    T = x2d.shape[0]
    scores = jax.nn.sigmoid((x2d @ w_router).astype(jnp.float32))
    biased = scores + router_bias.astype(jnp.float32)
    grp = biased.reshape(T, N_GROUPS, N_EXPERTS // N_GROUPS)
    grp_score = jnp.sum(lax.top_k(grp, 2)[0], -1)
    grp_idx = lax.top_k(grp_score, TOPK_GROUPS)[1]
    grp_mask = jnp.any(grp_idx[..., None] == jnp.arange(N_GROUPS), axis=-2)
    masked = jnp.where(jnp.repeat(grp_mask, N_EXPERTS // N_GROUPS, axis=-1), biased, -jnp.inf)
    idx = lax.top_k(masked, TOP_K)[1]
    w = jnp.take_along_axis(scores, idx, -1)
    w = w / jnp.sum(w, -1, keepdims=True) * ROUTED_SCALE
    return idx, w


def _routed_experts(x2d, idx, w, w_gate, w_up, w_down):
    T, D = x2d.shape
    A = T * TOP_K
    n_blocks = A // EXPERT_BLOCK + N_EXPERTS
    flat_e = idx.reshape(-1)
    flat_tok = jnp.repeat(jnp.arange(T, dtype=jnp.int32), TOP_K)
    flat_w = w.reshape(-1)
    order = jnp.argsort(flat_e)
    e_sorted = flat_e[order]
    counts = jnp.bincount(flat_e, length=N_EXPERTS)
    padded = (counts + EXPERT_BLOCK - 1) // EXPERT_BLOCK * EXPERT_BLOCK
    pad_end = jnp.cumsum(padded)
    pad_start = pad_end - padded
    start = jnp.cumsum(counts) - counts
    dest = pad_start[e_sorted] + jnp.arange(A) - start[e_sorted]
    slot_tok = jnp.full((n_blocks * EXPERT_BLOCK,), T, jnp.int32).at[dest].set(flat_tok[order])
    slot_w = jnp.zeros((n_blocks * EXPERT_BLOCK,), jnp.float32).at[dest].set(flat_w[order])
    block_e = jnp.minimum(jnp.searchsorted(pad_end, jnp.arange(n_blocks) * EXPERT_BLOCK, side='right'), N_EXPERTS - 1)
    x_pad = jnp.concatenate([x2d, jnp.zeros((1, D), x2d.dtype)], 0)

    def block(args):
        tok, e = args
        xb = x_pad[tok]
        h = jax.nn.silu(xb @ w_gate[e]) * (xb @ w_up[e])
        return h @ w_down[e]

    out = lax.map(block, (slot_tok.reshape(n_blocks, EXPERT_BLOCK), block_e))
    y = jnp.zeros((T + 1, D), jnp.float32).at[slot_tok].add(out.reshape(-1, D).astype(jnp.float32) * slot_w[:, None])
    return y[:T].astype(x2d.dtype)


def _swiglu(x, wg, wu, wd):
    return (jax.nn.silu(x @ wg) * (x @ wu)) @ wd


def _finish(x, p, oa, of, g_a, g_b, w_branch_a, w_branch_b, w_out, ln1_g, ln1_b, w_router, router_bias,
            w_exp_gate, w_exp_up, w_exp_down, w_sh_gate, w_sh_up, w_sh_down, w_ple_proj, w_ple_gate, ln2_g, ln2_b):
    mix = (g_a * (oa @ w_branch_a) + g_b * (of @ w_branch_b)) @ w_out
    h = _layer_norm(DEEPNORM_ALPHA * x + mix, ln1_g, ln1_b)
    B, T, D = h.shape
    h2 = h.reshape(B * T, D)
    idx, gw = _route(h2, w_router, router_bias)
    ffn = _routed_experts(h2, idx, gw, w_exp_gate, w_exp_up, w_exp_down) + _swiglu(h2, w_sh_gate, w_sh_up, w_sh_down)
    ple = jax.nn.sigmoid(h @ w_ple_gate) * (p @ w_ple_proj)
    return _layer_norm(DEEPNORM_ALPHA * h + ffn.reshape(B, T, D) + ple, ln2_g, ln2_b)


def setup_inputs(seed: int = 0) -> dict:
    key = jax.random.key(seed)
    keys = jax.random.split(key, 40)
    f32 = jnp.float32

    def nrm(i, shape, scale=1.0):
        return jax.random.normal(keys[i], shape, f32) * scale

    n_pages = PAST_LEN // PAGE_SIZE
    n_pool = (DEC_BATCH * n_pages * 5) // 4
    beta = DEEPNORM_BETA
    col_scale = jnp.concatenate([jnp.full((w,), beta if j in (2, 5) else 1.0, f32) for j, w in enumerate(IN_SPLITS)])
    page_table = jax.random.permutation(keys[0], n_pool)[: DEC_BATCH * n_pages].reshape(DEC_BATCH, n_pages).astype(jnp.int32)
    return {
        'x_prompt': nrm(1, (BATCH, SEQ, D_MODEL)),
        'x_sample': nrm(2, (DEC_BATCH, DEC_SEQ, D_MODEL)),
        'cache_da_k': nrm(3, (DEPTH, n_pool, PAGE_SIZE, DA_KV_HEADS, 2, DA_HEAD_DIM)),
        'cache_da_v': nrm(4, (DEPTH, n_pool, PAGE_SIZE, DA_KV_HEADS, 2 * DA_HEAD_DIM)),
        'cache_fox_k': nrm(5, (DEPTH, n_pool, PAGE_SIZE, FOX_KV_HEADS, FOX_HEAD_DIM)),
        'cache_fox_v': nrm(6, (DEPTH, n_pool, PAGE_SIZE, FOX_KV_HEADS, FOX_HEAD_DIM)),
        'cache_fox_logf': jax.nn.log_sigmoid(3.0 + nrm(7, (DEPTH, n_pool, PAGE_SIZE, FOX_HEADS))),
        'page_table': page_table,
        'p_prompt': nrm(8, (DEPTH, BATCH, SEQ, D_PLE)),
        'p_sample': nrm(9, (DEPTH, DEC_BATCH, DEC_SEQ, D_PLE)),
        'w_in': nrm(10, (DEPTH, D_MODEL, D_IN), D_MODEL ** -0.5) * col_scale,
        'b_forget': jnp.linspace(1.0, 5.0, FOX_HEADS, dtype=f32) + nrm(11, (DEPTH, FOX_HEADS), 0.1),
        'lambda_q1': nrm(12, (DEPTH, DA_HEAD_DIM), 0.1),
        'lambda_k1': nrm(13, (DEPTH, DA_HEAD_DIM), 0.1),
        'lambda_q2': nrm(14, (DEPTH, DA_HEAD_DIM), 0.1),
        'lambda_k2': nrm(15, (DEPTH, DA_HEAD_DIM), 0.1),
        'subln_g': 1.0 + nrm(16, (DEPTH, 2 * DA_HEAD_DIM), 0.02),
        'w_branch_a': nrm(17, (DEPTH, DA_OUT_W, D_MODEL), DA_OUT_W ** -0.5),
        'w_branch_b': nrm(18, (DEPTH, FOX_OUT_W, D_MODEL), FOX_OUT_W ** -0.5),
        'w_out': nrm(19, (DEPTH, D_MODEL, D_MODEL), beta * D_MODEL ** -0.5),
        'ln1_g': 1.0 + nrm(20, (DEPTH, D_MODEL), 0.02),
        'ln1_b': nrm(21, (DEPTH, D_MODEL), 0.02),
        'w_router': nrm(22, (DEPTH, D_MODEL, N_EXPERTS), D_MODEL ** -0.5),
        'router_bias': nrm(23, (DEPTH, N_EXPERTS), 0.01),
        'w_exp_gate': nrm(24, (DEPTH, N_EXPERTS, D_MODEL, D_EXPERT), beta * D_MODEL ** -0.5),
        'w_exp_up': nrm(25, (DEPTH, N_EXPERTS, D_MODEL, D_EXPERT), beta * D_MODEL ** -0.5),
        'w_exp_down': nrm(26, (DEPTH, N_EXPERTS, D_EXPERT, D_MODEL), beta * D_EXPERT ** -0.5),
        'w_sh_gate': nrm(27, (DEPTH, D_MODEL, D_SHARED), beta * D_MODEL ** -0.5),
        'w_sh_up': nrm(28, (DEPTH, D_MODEL, D_SHARED), beta * D_MODEL ** -0.5),
        'w_sh_down': nrm(29, (DEPTH, D_SHARED, D_MODEL), beta * D_SHARED ** -0.5),
        'w_ple_proj': nrm(30, (DEPTH, D_PLE, D_MODEL), beta * D_PLE ** -0.5),
        'w_ple_gate': nrm(31, (DEPTH, D_MODEL, D_MODEL), D_MODEL ** -0.5),
        'ln2_g': 1.0 + nrm(32, (DEPTH, D_MODEL), 0.02),
        'ln2_b': nrm(33, (DEPTH, D_MODEL), 0.02),
    }


def reference(x_prompt, x_sample, cache_da_k, cache_da_v, cache_fox_k, cache_fox_v, cache_fox_logf, page_table,
              p_prompt, p_sample, w_in, b_forget, lambda_q1, lambda_k1, lambda_q2, lambda_k2, subln_g,
              w_branch_a, w_branch_b, w_out, ln1_g, ln1_b, w_router, router_bias, w_exp_gate, w_exp_up, w_exp_down,
              w_sh_gate, w_sh_up, w_sh_down, w_ple_proj, w_ple_gate, ln2_g, ln2_b):
    past_len = page_table.shape[1] * cache_da_k.shape[2]
    pos_p = jnp.arange(x_prompt.shape[1])
    pos_s = past_len + jnp.arange(x_sample.shape[1])
    xp, xs = x_prompt, x_sample
    dakp, davp, fxkp, fxvp, lfp = [], [], [], [], []
    daks, davs, fxks, fxvs, lfs = [], [], [], [], []
    f32 = jnp.float32
    for l in range(DEPTH):
        lam_init = 0.8 - 0.6 * math.exp(-0.3 * l)
        lam = (jnp.exp(jnp.sum(lambda_q1[l].astype(f32) * lambda_k1[l].astype(f32)))
               - jnp.exp(jnp.sum(lambda_q2[l].astype(f32) * lambda_k2[l].astype(f32))) + lam_init)
        finish = functools.partial(
            _finish, w_branch_a=w_branch_a[l], w_branch_b=w_branch_b[l], w_out=w_out[l], ln1_g=ln1_g[l], ln1_b=ln1_b[l],
            w_router=w_router[l], router_bias=router_bias[l], w_exp_gate=w_exp_gate[l], w_exp_up=w_exp_up[l],
            w_exp_down=w_exp_down[l], w_sh_gate=w_sh_gate[l], w_sh_up=w_sh_up[l], w_sh_down=w_sh_down[l],
            w_ple_proj=w_ple_proj[l], w_ple_gate=w_ple_gate[l], ln2_g=ln2_g[l], ln2_b=ln2_b[l])
        qa, ka, va, qf, kf, vf, lf, ga, gb = _mixer_inputs(xp, w_in[l], b_forget[l], pos_p)
        oa, of = _mixers_prompt(qa, ka, va, qf, kf, vf, lf, lam, subln_g[l], lam_init)
        dakp.append(ka); davp.append(va); fxkp.append(kf); fxvp.append(vf); lfp.append(lf)
        xp = finish(xp, p_prompt[l], oa, of, ga, gb)
        qa, ka, va, qf, kf, vf, lf, ga, gb = _mixer_inputs(xs, w_in[l], b_forget[l], pos_s)
        oa, of = _mixers_sample(qa, ka, va, qf, kf, vf, lf,
                                _gather_pages(cache_da_k, l, page_table), _gather_pages(cache_da_v, l, page_table),
                                _gather_pages(cache_fox_k, l, page_table), _gather_pages(cache_fox_v, l, page_table),
                                _gather_pages(cache_fox_logf, l, page_table), lam, subln_g[l], lam_init)
        daks.append(ka); davs.append(va); fxks.append(kf); fxvs.append(vf); lfs.append(lf)
        xs = finish(xs, p_sample[l], oa, of, ga, gb)
    return (xp, xs, jnp.stack(dakp), jnp.stack(davp), jnp.stack(fxkp), jnp.stack(fxvp), jnp.stack(lfp),
            jnp.stack(daks), jnp.stack(davs), jnp.stack(fxks), jnp.stack(fxvs), jnp.stack(lfs))
```

```python
import functools
import math

import jax
import jax.numpy as jnp
from jax import lax
from jax.experimental import pallas as pl
from jax.experimental.pallas import tpu as pltpu

F32, BF16, I32 = jnp.float32, jnp.bfloat16, jnp.int32
SDS = jax.ShapeDtypeStruct

HEAD_DIM = 64
DA_KV, DA_G = 4, 2
FX_KV, FX_G = 4, 4
FX_HEADS = FX_KV * FX_G
ROPE_THETA = 10000.0
N_EXPERTS, TOP_K, N_GROUPS, TOPK_GROUPS = 256, 8, 8, 4
GROUP_SIZE = N_EXPERTS // N_GROUPS
ROUTED_SCALE = 2.5
EPS = 1e-5
QDA_W = DA_KV * DA_G * 2 * HEAD_DIM
KDA_W = DA_KV * 2 * HEAD_DIM
VDA_W = DA_KV * 2 * HEAD_DIM
QFX_W = FX_HEADS * HEAD_DIM
KFX_W = FX_KV * HEAD_DIM
VFX_W = FX_KV * HEAD_DIM
QKV_W = QDA_W + KDA_W + VDA_W + QFX_W + KFX_W + VFX_W

LANES = 128
SUBLANES = 8
VMEM_LIMIT_BYTES = 56 * 1024 * 1024

TOKEN_TILE = 256
Q_TILE = 256
PAGES_PER_STEP = 8
EXPERT_ROWS = 128
COMBINE_TILE = 128


def _cparams(sem):
    return pltpu.CompilerParams(dimension_semantics=sem, vmem_limit_bytes=VMEM_LIMIT_BYTES)


def _full(shape):
    n = len(shape)
    return pl.BlockSpec(shape, lambda *_: (0,) * n)


def _sigmoid(x):
    return 1.0 / (1.0 + jnp.exp(-x))


def _dot(a, b):
    return jnp.dot(a, b, preferred_element_type=F32)


def _dot_nt(a, b):
    return lax.dot_general(a, b, (((1,), (1,)), ((), ())), preferred_element_type=F32)


def _layer_norm(x, g, b):
    mu = jnp.mean(x, axis=-1, keepdims=True)
    xc = x - mu
    var = jnp.mean(xc * xc, axis=-1, keepdims=True)
    return xc * lax.rsqrt(var + EPS) * g + b


def _proj_kernel(x_ref, cos_ref, sin_ref, wqkv_ref, wf_ref, wg_ref, bf_ref,
                 qda_ref, kda_ref, kdab_ref, vda_ref, vdab_ref, qfx_ref, kfx_ref, vfx_ref, kvfx_ref,
                 logf_ref, nct_ref, gates_ref, carry_ref, *, tm):
    i = pl.program_id(1)
    xb = x_ref[0].astype(BF16)
    cos, sin = cos_ref[...], sin_ref[...]
    lane = lax.broadcasted_iota(I32, (tm, LANES), 1)
    first_half = (lane % HEAD_DIM) < (HEAD_DIM // 2)
    upper = lane >= HEAD_DIM
    scale = HEAD_DIM ** -0.5

    def rope(y):
        partner = jnp.where(first_half, pltpu.roll(y, LANES - HEAD_DIM // 2, 1), pltpu.roll(y, HEAD_DIM // 2, 1))
        return y * cos + partner * sin

    o = 0
    y = _dot(xb, wqkv_ref[:, o:o + QDA_W])
    for c in range(QDA_W // LANES):
        sl = slice(c * LANES, (c + 1) * LANES)
        qda_ref[0, :, sl] = (rope(y[:, sl]) * scale).astype(BF16)
    o += QDA_W
    y = _dot(xb, wqkv_ref[:, o:o + KDA_W])
    for c in range(KDA_W // LANES):
        sl = slice(c * LANES, (c + 1) * LANES)
        r = rope(y[:, sl])
        kda_ref[0, :, sl] = r
        kdab_ref[0, :, sl] = r.astype(BF16)
    o += KDA_W
    y = _dot(xb, wqkv_ref[:, o:o + VDA_W])
    vda_ref[0] = y
    vdab_ref[0] = y.astype(BF16)
    o += VDA_W
    y = _dot(xb, wqkv_ref[:, o:o + QFX_W])
    qfx_ref[0] = (y * scale).astype(BF16)
    o += QFX_W
    yk = _dot(xb, wqkv_ref[:, o:o + KFX_W])
    kfx_ref[0] = yk
    o += KFX_W
    yv = _dot(xb, wqkv_ref[:, o:o + VFX_W])
    vfx_ref[0] = yv
    for c in range(KFX_W // LANES):
        sl = slice(c * LANES, (c + 1) * LANES)
        kk, vv = yk[:, sl], yv[:, sl]
        even = jnp.where(upper, pltpu.roll(vv, HEAD_DIM, 1), kk)
        odd = jnp.where(upper, vv, pltpu.roll(kk, HEAD_DIM, 1))
        kvfx_ref[0, :, (2 * c) * LANES:(2 * c + 1) * LANES] = even.astype(BF16)
        kvfx_ref[0, :, (2 * c + 1) * LANES:(2 * c + 2) * LANES] = odd.astype(BF16)

    f = _dot(xb, wf_ref[...]) + bf_ref[...]
    logf = jnp.minimum(f, 0.0) - jnp.log1p(jnp.exp(-jnp.abs(f)))
    logf_ref[0] = logf[:, :FX_HEADS]

    @pl.when(i == 0)
    def _():
        carry_ref[...] = jnp.zeros_like(carry_ref)

    row = lax.broadcasted_iota(I32, (tm, LANES), 0)
    cum = logf
    d = 1
    while d < tm:
        cum = cum + jnp.where(row >= d, pltpu.roll(cum, d, 0), 0.0)
        d *= 2
    cum = cum + carry_ref[0:1, :]
    carry_ref[...] = jnp.broadcast_to(cum[tm - 1:tm, :], carry_ref.shape)
    nct = -cum.T
    for h in range(FX_KV):
        nct_ref[0, 0, h] = nct[h * FX_G:(h + 1) * FX_G, :]

    gates_ref[0] = _sigmoid(_dot(xb, wg_ref[...]))


def _proj_call(x, cos, sin, wqkv, wf, wg, bf, *, tm):
    B, T, D = x.shape
    nt = T // tm
    row = lambda b, i: (b, i, 0)
    tab = pl.BlockSpec((tm, LANES), lambda b, i: (i, 0))
    widths_dtypes = [(QDA_W, BF16), (KDA_W, F32), (KDA_W, BF16), (VDA_W, F32), (VDA_W, BF16), (QFX_W, BF16),
                     (KFX_W, F32), (VFX_W, F32), (FX_KV * LANES, BF16), (FX_HEADS, F32)]
    out_shape = [SDS((B, T, w), dt) for w, dt in widths_dtypes]
    out_specs = [pl.BlockSpec((1, tm, w), row) for w, _ in widths_dtypes]
    out_shape += [SDS((B, nt, FX_KV, FX_G, tm), F32), SDS((B, T, 2 * D), F32)]
    out_specs += [pl.BlockSpec((1, 1, FX_KV, FX_G, tm), lambda b, i: (b, i, 0, 0, 0)), pl.BlockSpec((1, tm, 2 * D), row)]
    return pl.pallas_call(
        functools.partial(_proj_kernel, tm=tm),
        grid=(B, nt),
        in_specs=[pl.BlockSpec((1, tm, D), row), tab, tab, _full(wqkv.shape), _full(wf.shape), _full(wg.shape),
                  _full(bf.shape)],
        out_specs=out_specs,
        out_shape=out_shape,
        scratch_shapes=[pltpu.VMEM((SUBLANES, LANES), F32)],
        compiler_params=_cparams(("arbitrary", "arbitrary")),
        name="proj",
    )(x, cos, sin, wqkv, wf, wg, bf)


def _rope_tables(pos):
    half = HEAD_DIM // 2
    inv = ROPE_THETA ** (-jnp.arange(0, HEAD_DIM, 2, dtype=F32) / HEAD_DIM)
    ang = pos.astype(F32)[:, None] * inv[None, :]
    cos, sin = jnp.cos(ang), jnp.sin(ang)
    reps = LANES // half
    cos_t = jnp.tile(cos, (1, reps))
    sin_t = jnp.concatenate([-sin, sin] * (reps // 2), axis=1)
    return cos_t, sin_t


def _attn_kernel(*refs, fox, tq, tk, lam_init):
    if fox:
        q_ref, kv_ref, nct_ref, o_ref, qs_ref, m_ref, l_ref, acc_ref = refs
        k_ref = v_ref = kv_ref
    else:
        lam_ref, g_ref, q_ref, k_ref, v_ref, o_ref, qs_ref, m_ref, l_ref, acc_ref = refs
    qi = pl.program_id(2)
    nrow = 4 * tq
    lane = lax.broadcasted_iota(I32, (tq, LANES), 1)
    lower = lane < HEAD_DIM

    if fox:
        qf = q_ref[0].astype(F32)
        for g in range(FX_G):
            slab = qf[:, (g // 2) * LANES:(g // 2 + 1) * LANES]
            if g % 2:
                slab = pltpu.roll(slab, HEAD_DIM, 1)
            qs_ref[g * tq:(g + 1) * tq, :] = jnp.where(lower, slab, 0.0).astype(BF16)
    else:
        q = q_ref[0]
        zero = jnp.zeros_like(q[:, :LANES])
        for g in range(DA_G):
            slab = q[:, g * LANES:(g + 1) * LANES]
            qs_ref[(2 * g) * tq:(2 * g + 1) * tq, :] = jnp.where(lower, slab, zero)
            qs_ref[(2 * g + 1) * tq:(2 * g + 2) * tq, :] = jnp.where(lower, zero, slab)

    m_ref[...] = jnp.full_like(m_ref, -jnp.inf)
    l_ref[...] = jnp.zeros_like(l_ref)
    acc_ref[...] = jnp.zeros_like(acc_ref)

    def block(kb, masked):
        start = pl.multiple_of(kb * tk, tk)
        k = k_ref[0, pl.ds(start, tk), :]
        s = _dot_nt(qs_ref[...], k)
        if fox:
            nc = nct_ref[0, kb, 0]
            s = jnp.concatenate([s[g * tq:(g + 1) * tq] + nc[g:g + 1, :] for g in range(FX_G)], axis=0)
        if masked:
            r = lax.broadcasted_iota(I32, (nrow, tk), 0) % tq
            c = lax.broadcasted_iota(I32, (nrow, tk), 1)
            s = jnp.where(c <= r, s, -jnp.inf)
        m_prev = m_ref[...]
        m_new = jnp.maximum(m_prev, jnp.max(s, axis=1, keepdims=True))
        alpha = jnp.exp(m_prev - m_new)
        p = jnp.exp(s - m_new)
        l_ref[...] = alpha * l_ref[...] + jnp.sum(p, axis=1, keepdims=True)
        v = v_ref[0, pl.ds(start, tk), :]
        acc_ref[...] = alpha * acc_ref[...] + _dot(p.astype(BF16), v)
        m_ref[...] = m_new

    def body(kb, carry):
        block(kb, False)
        return carry

    lax.fori_loop(0, qi, body, 0)
    block(qi, True)

    o = acc_ref[...] / l_ref[...]
    if fox:
        for c in range(FX_G // 2):
            a = o[(2 * c) * tq:(2 * c + 1) * tq]
            b = o[(2 * c + 1) * tq:(2 * c + 2) * tq]
            o_ref[0, :, c * LANES:(c + 1) * LANES] = jnp.where(lower, pltpu.roll(a, HEAD_DIM, 1), b).astype(o_ref.dtype)
    else:
        lam = lam_ref[0]
        for g in range(DA_G):
            og = o[(2 * g) * tq:(2 * g + 1) * tq] - lam * o[(2 * g + 1) * tq:(2 * g + 2) * tq]
            og = og * lax.rsqrt(jnp.mean(og * og, axis=-1, keepdims=True) + EPS) * g_ref[...] * (1.0 - lam_init)
            o_ref[0, :, g * LANES:(g + 1) * LANES] = og.astype(o_ref.dtype)


def _attn_call(q, k, v, nct, lam, subln, *, fox, tq, tk, lam_init):
    assert tq == tk
    B, T, _ = q.shape
    nk = T // tk
    qspec = pl.BlockSpec((1, tq, 2 * LANES), lambda b, h, i: (b, i, h))
    kvspec = pl.BlockSpec((1, T, LANES), lambda b, h, i: (b, 0, h))
    if fox:
        args = (q, k, nct)
        in_specs = [qspec, kvspec, pl.BlockSpec((1, nk, 1, FX_G, tk), lambda b, h, i: (b, 0, h, 0, 0))]
    else:
        args = (lam, subln, q, k, v)
        in_specs = [pl.BlockSpec(memory_space=pltpu.SMEM), _full(subln.shape), qspec, kvspec, kvspec]
    return pl.pallas_call(
        functools.partial(_attn_kernel, fox=fox, tq=tq, tk=tk, lam_init=lam_init),
        grid=(B, 4, T // tq),
        in_specs=in_specs,
        out_specs=qspec,
        out_shape=SDS(q.shape, BF16),
        scratch_shapes=[pltpu.VMEM((4 * tq, LANES), BF16), pltpu.VMEM((4 * tq, 1), F32), pltpu.VMEM((4 * tq, 1), F32),
                        pltpu.VMEM((4 * tq, LANES), F32)],
        compiler_params=_cparams(("arbitrary", "arbitrary", "arbitrary")),
        name="attn_fox" if fox else "attn_da",
    )(*args)


def _lane_cumsum(x):
    lane = lax.broadcasted_iota(I32, x.shape, 1)
    d = 1
    while d < x.shape[1]:
        x = x + jnp.where(lane >= d, pltpu.roll(x, d, 1), 0.0)
        d *= 2
    return x


def _sample_kernel(pt_ref, lam_ref, g_ref, qda_ref, qfx_ref, kdan_ref, vdan_ref, kfxn_ref, vfxn_ref, lfn_ref, *rest,
                   npg, nsteps, tq, lam_init):
    kda = rest[0 * npg:1 * npg]
    vda = rest[1 * npg:2 * npg]
    kfx = rest[2 * npg:3 * npg]
    vfx = rest[3 * npg:4 * npg]
    lfp = rest[4 * npg:5 * npg]
    (oa_ref, of_ref, qbda_ref, qbfx_ref, mda_ref, lda_ref, accda_ref, mfx_ref, lfx_ref, accfx_ref,
     carry_ref, pad_ref) = rest[5 * npg:]
    j = pl.program_id(1)
    nr = LANES
    lane8 = lax.broadcasted_iota(I32, (tq, LANES), 1)
    rows_per_head_da = nr // DA_KV

    @pl.when(j == 0)
    def _():
        qbda_ref[...] = jnp.zeros_like(qbda_ref)
        qbfx_ref[...] = jnp.zeros_like(qbfx_ref)
        for h in range(DA_KV):
            for g in range(DA_G):
                slab = qda_ref[:, (h * DA_G + g) * LANES:(h * DA_G + g + 1) * LANES]
                for m in range(2):
                    rb = (h * DA_G + g) * 2 + m
                    qbda_ref[rb * tq:(rb + 1) * tq, h * LANES:(h + 1) * LANES] = jnp.where(
                        (lane8 // HEAD_DIM) == m, slab, 0.0)
        for h in range(FX_KV):
            for g in range(FX_G):
                head = h * FX_G + g
                slab = qfx_ref[:, (head // 2) * LANES:(head // 2 + 1) * LANES]
                if (head % 2) != (h % 2):
                    slab = pltpu.roll(slab, HEAD_DIM, 1)
                qbfx_ref[head * tq:(head + 1) * tq, (h // 2) * LANES:(h // 2 + 1) * LANES] = jnp.where(
                    (lane8 // HEAD_DIM) == (h % 2), slab, 0.0)
        for r in (mda_ref, mfx_ref):
            r[...] = jnp.full_like(r, -jnp.inf)
        for r in (lda_ref, lfx_ref, accda_ref, accfx_ref, carry_ref):
            r[...] = jnp.zeros_like(r)

    qbda = qbda_ref[...].astype(BF16)
    qbfx = qbfx_ref[...].astype(BF16)

    def softmax_update(s, m_ref, l_ref):
        m_prev = m_ref[...]
        m_new = jnp.maximum(m_prev, jnp.max(s, axis=1, keepdims=True))
        alpha = jnp.exp(m_prev - m_new)
        p = jnp.exp(s - m_new)
        l_ref[...] = alpha * l_ref[...] + jnp.sum(p, axis=1, keepdims=True)
        m_ref[...] = m_new
        return alpha, p.astype(BF16)

    def fox_bias(s, negc):
        return jnp.concatenate([s[hd * tq:(hd + 1) * tq] + negc[hd:hd + 1, :] for hd in range(FX_HEADS)], axis=0)

    s_da = jnp.concatenate([_dot(qbda, kda[p][...].astype(BF16)) for p in range(npg)], axis=1)
    alpha, pr = softmax_update(s_da, mda_ref, lda_ref)
    for h in range(DA_KV):
        rs = slice(h * rows_per_head_da, (h + 1) * rows_per_head_da)
        pv = 0.0
        for p in range(npg):
            vh = vda[p][pl.ds(h, LANES, stride=DA_KV), :].astype(BF16)
            pv = pv + _dot(pr[rs, p * LANES:(p + 1) * LANES], vh)
        accda_ref[rs, :] = alpha[rs] * accda_ref[rs, :] + pv

    s_list = []
    for p in range(npg):
        cum = _lane_cumsum(lfp[p][...]) + carry_ref[...]
        carry_ref[...] = jnp.broadcast_to(cum[:, LANES - 1:LANES], carry_ref.shape)
        s_list.append(fox_bias(_dot(qbfx, kfx[p][...].astype(BF16)), -cum))
    alpha, pr = softmax_update(jnp.concatenate(s_list, axis=1), mfx_ref, lfx_ref)
    pv = 0.0
    for p in range(npg):
        pv = pv + _dot_nt(pr[:, p * LANES:(p + 1) * LANES], vfx[p][...].astype(BF16))
    accfx_ref[...] = alpha * accfx_ref[...] + pv

    @pl.when(j == nsteps - 1)
    def _():
        rowq = lax.broadcasted_iota(I32, (nr, LANES), 0) % tq
        col = lax.broadcasted_iota(I32, (nr, LANES), 1)
        valid = (col < tq) & (col <= rowq)

        def padded(x):
            w = x.shape[1]
            wp = max(w, LANES)
            pad_ref[:, :wp] = jnp.zeros((LANES, wp), F32)
            pad_ref[0:tq, :w] = x
            return pad_ref[:, :wp]

        s = jnp.where(valid, _dot_nt(qbda, padded(kdan_ref[...]).astype(BF16)), -jnp.inf)
        alpha, pr = softmax_update(s, mda_ref, lda_ref)
        vpad = padded(vdan_ref[...]).astype(BF16)
        for h in range(DA_KV):
            rs = slice(h * rows_per_head_da, (h + 1) * rows_per_head_da)
            accda_ref[rs, :] = alpha[rs] * accda_ref[rs, :] + _dot(pr[rs], vpad[:, h * LANES:(h + 1) * LANES])

        lft = padded(lfn_ref[...]).T
        cum = _lane_cumsum(lft[:FX_HEADS]) + carry_ref[...]
        s = fox_bias(_dot_nt(qbfx, padded(kfxn_ref[...]).astype(BF16)), -cum)
        s = jnp.where(valid, s, -jnp.inf)
        alpha, pr = softmax_update(s, mfx_ref, lfx_ref)
        accfx_ref[...] = alpha * accfx_ref[...] + _dot(pr, padded(vfxn_ref[...]).astype(BF16))

        lam = lam_ref[0]
        oda = accda_ref[...] / lda_ref[...]
        for h in range(DA_KV):
            for g in range(DA_G):
                rb = (h * DA_G + g) * 2
                og = oda[rb * tq:(rb + 1) * tq] - lam * oda[(rb + 1) * tq:(rb + 2) * tq]
                og = og * lax.rsqrt(jnp.mean(og * og, axis=-1, keepdims=True) + EPS) * g_ref[...] * (1.0 - lam_init)
                oa_ref[:, (h * DA_G + g) * LANES:(h * DA_G + g + 1) * LANES] = og
        ofx = accfx_ref[...] / lfx_ref[...]
        for sl in range(FX_HEADS // 2):
            pieces = []
            for head in (2 * sl, 2 * sl + 1):
                h = head // FX_G
                piece = ofx[head * tq:(head + 1) * tq, (h // 2) * LANES:(h // 2 + 1) * LANES]
                if (head % 2) != (h % 2):
                    piece = pltpu.roll(piece, HEAD_DIM, 1)
                pieces.append(piece)
            of_ref[:, sl * LANES:(sl + 1) * LANES] = jnp.where(lane8 < HEAD_DIM, pieces[0], pieces[1])


def _sample_call(page_table, lam, subln, qda, qfx, kdan, vdan, kfxn, vfxn, lfn, kda_c, vda_c, kfx_c, vfx_c, lf_c,
                 *, npg, lam_init):
    S, tq, _ = qda.shape
    n_pages = page_table.shape[1]
    nsteps = n_pages // npg
    assert nsteps * npg == n_pages and FX_HEADS * tq == LANES

    def new(w):
        return pl.BlockSpec((None, tq, w), lambda s, j, pt: (s, 0, 0))

    def page(rows, p):
        return pl.BlockSpec((None, rows, LANES), lambda s, j, pt: (pt[s, j * npg + p], 0, 0))

    in_specs = [pl.BlockSpec(memory_space=pltpu.SMEM), pl.BlockSpec(subln.shape, lambda s, j, pt: (0, 0)),
                new(QDA_W), new(QFX_W), new(KDA_W), new(VDA_W), new(KFX_W), new(VFX_W), new(FX_HEADS)]
    args = [lam, subln, qda, qfx, kdan, vdan, kfxn, vfxn, lfn]
    for arr in (kda_c, vda_c, kfx_c, vfx_c, lf_c):
        for p in range(npg):
            in_specs.append(page(arr.shape[1], p))
            args.append(arr)
    out_spec = pl.BlockSpec((None, tq, QDA_W), lambda s, j, pt: (s, 0, 0))
    grid_spec = pltpu.PrefetchScalarGridSpec(
        num_scalar_prefetch=1,
        grid=(S, nsteps),
        in_specs=in_specs,
        out_specs=[out_spec, out_spec],
        scratch_shapes=[pltpu.VMEM((LANES, KDA_W), F32), pltpu.VMEM((LANES, KFX_W), F32),
                        pltpu.VMEM((LANES, 1), F32), pltpu.VMEM((LANES, 1), F32), pltpu.VMEM((LANES, LANES), F32),
                        pltpu.VMEM((LANES, 1), F32), pltpu.VMEM((LANES, 1), F32), pltpu.VMEM((LANES, KFX_W), F32),
                        pltpu.VMEM((FX_HEADS, LANES), F32), pltpu.VMEM((LANES, KDA_W), F32)],
    )
    return pl.pallas_call(
        functools.partial(_sample_kernel, npg=npg, nsteps=nsteps, tq=tq, lam_init=lam_init),
        grid_spec=grid_spec,
        out_shape=[SDS((S, tq, QDA_W), F32), SDS((S, tq, QFX_W), F32)],
        compiler_params=_cparams(("arbitrary", "arbitrary")),
        name="sample_attn",
    )(page_table, *args)


def _mix_kernel(oa_ref, of_ref, gates_ref, x_ref, wa_ref, wb_ref, wo_ref, g_ref, b_ref, h_ref, *, alpha):
    d = x_ref.shape[-1]
    a = _dot(oa_ref[...].astype(BF16), wa_ref[...])
    b = _dot(of_ref[...].astype(BF16), wb_ref[...])
    merged = gates_ref[:, :d] * a + gates_ref[:, d:] * b
    mix = _dot(merged.astype(BF16), wo_ref[...])
    h_ref[...] = _layer_norm(alpha * x_ref[...] + mix, g_ref[...], b_ref[...])


def _mix_call(oa, of, gates, x, wa, wb, wo, g, b, *, alpha, tm):
    n, d = x.shape
    row = lambda i: (i, 0)
    return pl.pallas_call(
        functools.partial(_mix_kernel, alpha=alpha),
        grid=(n // tm,),
        in_specs=[pl.BlockSpec((tm, oa.shape[1]), row), pl.BlockSpec((tm, of.shape[1]), row),
                  pl.BlockSpec((tm, 2 * d), row), pl.BlockSpec((tm, d), row),
                  _full(wa.shape), _full(wb.shape), _full(wo.shape), _full(g.shape), _full(b.shape)],
        out_specs=pl.BlockSpec((tm, d), row),
        out_shape=SDS((n, d), F32),
        compiler_params=_cparams(("arbitrary",)),
        name="mix_ln1",
    )(oa, of, gates, x, wa, wb, wo, g, b)


def _route(scores, biased):
    n_exp, tm = scores.shape
    rid = lax.broadcasted_iota(I32, (n_exp, tm), 0)
    gr = lax.broadcasted_iota(I32, (GROUP_SIZE, tm), 0)
    neg = -jnp.inf
    gscore = []
    for g in range(N_GROUPS):
        v = biased[g * GROUP_SIZE:(g + 1) * GROUP_SIZE]
        m1 = jnp.max(v, axis=0, keepdims=True)
        i1 = jnp.min(jnp.where(v == m1, gr, GROUP_SIZE), axis=0, keepdims=True)
        m2 = jnp.max(jnp.where(gr == i1, neg, v), axis=0, keepdims=True)
        gscore.append(m1 + m2)
    chosen = [jnp.zeros((1, tm), I32) for _ in range(N_GROUPS)]
    for _ in range(TOPK_GROUPS):
        m = functools.reduce(jnp.maximum, gscore)
        first = functools.reduce(jnp.minimum, [jnp.where(gscore[g] == m, g, N_GROUPS) for g in range(N_GROUPS)])
        for g in range(N_GROUPS):
            hit = first == g
            chosen[g] = jnp.where(hit, 1, chosen[g])
            gscore[g] = jnp.where(hit, neg, gscore[g])
    cur = jnp.concatenate(
        [jnp.where(chosen[g] > 0, biased[g * GROUP_SIZE:(g + 1) * GROUP_SIZE], neg) for g in range(N_GROUPS)], axis=0)
    kid = lax.broadcasted_iota(I32, (TOP_K, tm), 0)
    idx = jnp.zeros((TOP_K, tm), I32)
    w = jnp.zeros((TOP_K, tm), F32)
    for k in range(TOP_K):
        m = jnp.max(cur, axis=0, keepdims=True)
        i = jnp.min(jnp.where(cur == m, rid, n_exp), axis=0, keepdims=True)
        hit = rid == i
        wk = jnp.sum(jnp.where(hit, scores, 0.0), axis=0, keepdims=True)
        idx = jnp.where(kid == k, i, idx)
        w = jnp.where(kid == k, wk, w)
        cur = jnp.where(hit, neg, cur)
    w = w / jnp.sum(w, axis=0, keepdims=True) * ROUTED_SCALE
    return idx, w


def _ffn_kernel(h_ref, p_ref, wrt_ref, rb_ref, wsg_ref, wsu_ref, wsd_ref, wpp_ref, wpg_ref,
                base_ref, idx_ref, w_ref, *, alpha):
    h = h_ref[...]
    hb = h.astype(BF16)
    scores = _sigmoid(_dot_nt(wrt_ref[...], hb))
    idx, w = _route(scores, scores + rb_ref[...])
    idx_ref[...] = idx
    w_ref[...] = w
    gate = _dot(hb, wsg_ref[...])
    shared = _dot((gate * _sigmoid(gate) * _dot(hb, wsu_ref[...])).astype(BF16), wsd_ref[...])
    ple = _sigmoid(_dot(hb, wpg_ref[...])) * _dot(p_ref[...].astype(BF16), wpp_ref[...])
    base_ref[...] = alpha * h + shared + ple


def _ffn_call(h, p, wrt, rbias, wsg, wsu, wsd, wpp, wpg, *, alpha, tm):
    n, d = h.shape
    row = lambda i: (i, 0)
    col = lambda i: (0, i)
    return pl.pallas_call(
        functools.partial(_ffn_kernel, alpha=alpha),
        grid=(n // tm,),
        in_specs=[pl.BlockSpec((tm, d), row), pl.BlockSpec((tm, p.shape[1]), row), _full(wrt.shape), _full(rbias.shape),
                  _full(wsg.shape), _full(wsu.shape), _full(wsd.shape), _full(wpp.shape), _full(wpg.shape)],
        out_specs=[pl.BlockSpec((tm, d), row), pl.BlockSpec((TOP_K, tm), col), pl.BlockSpec((TOP_K, tm), col)],
        out_shape=[SDS((n, d), F32), SDS((TOP_K, n), I32), SDS((TOP_K, n), F32)],
        compiler_params=_cparams(("arbitrary",)),
        name="router_shared_ple",
    )(h, p, wrt, rbias, wsg, wsu, wsd, wpp, wpg)


def _gather_rows(idx_ref, n, src_hbm, dst, sem):
    def body(r, carry):
        pltpu.make_async_copy(src_hbm.at[idx_ref[0, r]], dst.at[r], sem).start()
        return carry
    lax.fori_loop(0, n, body, 0)


def _wait_rows(n, src_hbm, dst, sem):
    pltpu.make_async_copy(src_hbm.at[pl.ds(0, n)], dst, sem).wait()


def _moe_kernel(be_ref, tokc_ref, tokn_ref, sw_ref, h_hbm, wg_ref, wu_ref, wd_ref, o_ref, xbuf, sem, *, rb, nb):
    i = pl.program_id(0)
    slot = i % 2

    @pl.when(i == 0)
    def _():
        _gather_rows(tokc_ref, rb, h_hbm, xbuf.at[0], sem.at[0])

    @pl.when(i + 1 < nb)
    def _():
        _gather_rows(tokn_ref, rb, h_hbm, xbuf.at[1 - slot], sem.at[1 - slot])

    _wait_rows(rb, h_hbm, xbuf.at[slot], sem.at[slot])
    x = xbuf[slot].reshape(rb, xbuf.shape[-1]).astype(BF16)
    gate = _dot(x, wg_ref[...].astype(BF16))
    hid = gate * _sigmoid(gate) * _dot(x, wu_ref[...].astype(BF16))
    o_ref[...] = _dot(hid.astype(BF16), wd_ref[...].astype(BF16)) * sw_ref[...]


def _moe_call(block_e, slot_tok, slot_w, h_rows, wg, wu, wd, *, rb):
    nb = block_e.shape[0]
    d = h_rows.shape[-1]
    de = wg.shape[-1]
    tok3 = slot_tok.reshape(nb, 1, rb)
    sw3 = slot_w.reshape(nb, rb, 1)
    grid_spec = pltpu.PrefetchScalarGridSpec(
        num_scalar_prefetch=1,
        grid=(nb,),
        in_specs=[pl.BlockSpec((None, 1, rb), lambda i, be: (i, 0, 0), memory_space=pltpu.SMEM),
                  pl.BlockSpec((None, 1, rb), lambda i, be: (jnp.minimum(i + 1, nb - 1), 0, 0), memory_space=pltpu.SMEM),
                  pl.BlockSpec((None, rb, 1), lambda i, be: (i, 0, 0)),
                  pl.BlockSpec(memory_space=pl.ANY),
                  pl.BlockSpec((None, d, de), lambda i, be: (be[i], 0, 0)),
                  pl.BlockSpec((None, d, de), lambda i, be: (be[i], 0, 0)),
                  pl.BlockSpec((None, de, d), lambda i, be: (be[i], 0, 0))],
        out_specs=pl.BlockSpec((rb, d), lambda i, be: (i, 0)),
        scratch_shapes=[pltpu.VMEM((2, rb, 1, d), F32), pltpu.SemaphoreType.DMA((2,))],
    )
    return pl.pallas_call(
        functools.partial(_moe_kernel, rb=rb, nb=nb),
        grid_spec=grid_spec,
        out_shape=SDS((nb * rb, d), F32),
        compiler_params=_cparams(("arbitrary",)),
        name="moe_experts",
    )(block_e, tok3, tok3, sw3, h_rows, wg, wu, wd)


def _moe_metadata(idx, w, n, rb):
    a = TOP_K * n
    nb = a // rb + N_EXPERTS
    flat_e = idx.reshape(-1)
    order = jnp.argsort(flat_e)
    e_sorted = flat_e[order]
    counts = jnp.bincount(flat_e, length=N_EXPERTS)
    padded = (counts + rb - 1) // rb * rb
    pad_end = jnp.cumsum(padded)
    pad_start = pad_end - padded
    start = jnp.cumsum(counts) - counts
    dest = (pad_start[e_sorted] + jnp.arange(a) - start[e_sorted]).astype(I32)
    slot_tok = jnp.full((nb * rb,), n, I32).at[dest].set((order % n).astype(I32))
    slot_w = jnp.zeros((nb * rb,), F32).at[dest].set(w.reshape(-1)[order])
    pos = jnp.zeros((a,), I32).at[order].set(dest)
    block_e = jnp.minimum(jnp.searchsorted(pad_end, jnp.arange(nb) * rb, side="right"), N_EXPERTS - 1).astype(I32)
    return block_e, slot_tok, slot_w, pos.reshape(TOP_K, n)


def _combine_kernel(posc_ref, posn_ref, y_hbm, base_ref, g_ref, b_ref, o_ref, buf, acc2d, sem, *, tt, nt):
    i = pl.program_id(0)
    slot = i % 2
    nrows = TOP_K * tt

    @pl.when(i == 0)
    def _():
        _gather_rows(posc_ref, nrows, y_hbm, buf.at[0], sem.at[0])

    @pl.when(i + 1 < nt)
    def _():
        _gather_rows(posn_ref, nrows, y_hbm, buf.at[1 - slot], sem.at[1 - slot])

    _wait_rows(nrows, y_hbm, buf.at[slot], sem.at[slot])
    tot = buf[slot, 0:tt]
    for k in range(1, TOP_K):
        tot = tot + buf[slot, k * tt:(k + 1) * tt]
    acc2d[...] = tot.reshape(tt, acc2d.shape[-1])
    o_ref[...] = _layer_norm(base_ref[...] + acc2d[...], g_ref[...], b_ref[...])


def _combine_call(pos, y_rows, base, g, b, *, tt):
    n, d = base.shape
    nt = n // tt
    pos3 = pos.reshape(TOP_K, nt, tt).transpose(1, 0, 2).reshape(nt, 1, TOP_K * tt)
    row = lambda i: (i, 0)
    return pl.pallas_call(
        functools.partial(_combine_kernel, tt=tt, nt=nt),
        grid=(nt,),
        in_specs=[pl.BlockSpec((None, 1, TOP_K * tt), lambda i: (i, 0, 0), memory_space=pltpu.SMEM),
                  pl.BlockSpec((None, 1, TOP_K * tt), lambda i: (jnp.minimum(i + 1, nt - 1), 0, 0),
                               memory_space=pltpu.SMEM),
                  pl.BlockSpec(memory_space=pl.ANY),
                  pl.BlockSpec((tt, d), row), _full(g.shape), _full(b.shape)],
        out_specs=pl.BlockSpec((tt, d), row),
        out_shape=SDS((n, d), F32),
        scratch_shapes=[pltpu.VMEM((2, TOP_K * tt, 1, d), F32), pltpu.VMEM((tt, d), F32), pltpu.SemaphoreType.DMA((2,))],
        compiler_params=_cparams(("arbitrary",)),
        name="combine_ln2",
    )(pos3, pos3, y_rows, base, g, b)


def kernel(x_prompt, x_sample, cache_da_k, cache_da_v, cache_fox_k, cache_fox_v, cache_fox_logf, page_table, p_prompt, p_sample, w_in, b_forget, lambda_q1, lambda_k1, lambda_q2, lambda_k2, subln_g, w_branch_a, w_branch_b, w_out, ln1_g, ln1_b, w_router, router_bias, w_exp_gate, w_exp_up, w_exp_down, w_sh_gate, w_sh_up, w_sh_down, w_ple_proj, w_ple_gate, ln2_g, ln2_b):
    depth = w_in.shape[0]
    alpha = (2 * depth) ** 0.25
    B, T, D = x_prompt.shape
    S, TQ, _ = x_sample.shape
    n_pool, page_size = cache_da_k.shape[1], cache_da_k.shape[2]
    past_len = page_table.shape[1] * page_size
    n_p, n_s = B * T, S * TQ
    n = n_p + n_s
    cos_p, sin_p = _rope_tables(jnp.arange(T))
    cos_s, sin_s = _rope_tables(jnp.tile(past_len + jnp.arange(TQ), S))
    xp, xs = x_prompt, x_sample
    outs_p, outs_s = [[] for _ in range(5)], [[] for _ in range(5)]
    for l in range(depth):
        lam_init = 0.8 - 0.6 * math.exp(-0.3 * l)
        lam = (jnp.exp(jnp.sum(lambda_q1[l] * lambda_k1[l])) - jnp.exp(jnp.sum(lambda_q2[l] * lambda_k2[l]))
               + lam_init).reshape(1).astype(F32)
        subln = subln_g[l].reshape(1, -1)
        wl = w_in[l]
        wqkv = wl[:, :QKV_W].astype(BF16)
        wf = jnp.pad(wl[:, QKV_W:QKV_W + FX_HEADS], ((0, 0), (0, LANES - FX_HEADS))).astype(BF16)
        wg = wl[:, QKV_W + FX_HEADS:].astype(BF16)
        bf = jnp.pad(b_forget[l], (0, LANES - FX_HEADS)).reshape(1, LANES)

        (qda, kda, kdab, vda, vdab, qfx, kfx, vfx, kvfx, logf, nct, gates) = _proj_call(
            xp, cos_p, sin_p, wqkv, wf, wg, bf, tm=TOKEN_TILE)
        oa_p = _attn_call(qda, kdab, vdab, None, lam, subln, fox=False, tq=Q_TILE, tk=TOKEN_TILE, lam_init=lam_init)
        of_p = _attn_call(qfx, kvfx, None, nct, None, None, fox=True, tq=Q_TILE, tk=TOKEN_TILE, lam_init=lam_init)
        for dst, arr in zip(outs_p, (kda.reshape(B, T, DA_KV, 2, HEAD_DIM), vda.reshape(B, T, DA_KV, 2 * HEAD_DIM),
                                     kfx.reshape(B, T, FX_KV, HEAD_DIM), vfx.reshape(B, T, FX_KV, HEAD_DIM), logf)):
            dst.append(arr)
        gates_p = gates.reshape(n_p, 2 * D)

        (qda, kda, _, vda, _, qfx, kfx, vfx, _, logf, _, gates) = _proj_call(
            xs.reshape(1, n_s, D), cos_s, sin_s, wqkv, wf, wg, bf, tm=TOKEN_TILE)
        per_seq = lambda a: a.reshape(S, TQ, a.shape[-1])
        kT_da = jnp.transpose(cache_da_k[l], (0, 2, 3, 4, 1)).reshape(n_pool, KDA_W, page_size)
        v_da = cache_da_v[l].reshape(n_pool, page_size * DA_KV, 2 * HEAD_DIM)
        kT_fx = jnp.transpose(cache_fox_k[l], (0, 2, 3, 1)).reshape(n_pool, KFX_W, page_size)
        vT_fx = jnp.transpose(cache_fox_v[l], (0, 2, 3, 1)).reshape(n_pool, VFX_W, page_size)
        lfT = jnp.transpose(cache_fox_logf[l], (0, 2, 1))
        oa_s, of_s = _sample_call(page_table, lam, subln, per_seq(qda).astype(F32), per_seq(qfx).astype(F32),
                                  per_seq(kda), per_seq(vda), per_seq(kfx), per_seq(vfx), per_seq(logf),
                                  kT_da, v_da, kT_fx, vT_fx, lfT, npg=PAGES_PER_STEP, lam_init=lam_init)
        for dst, arr in zip(outs_s, (kda.reshape(S, TQ, DA_KV, 2, HEAD_DIM), vda.reshape(S, TQ, DA_KV, 2 * HEAD_DIM),
                                     kfx.reshape(S, TQ, FX_KV, HEAD_DIM), vfx.reshape(S, TQ, FX_KV, HEAD_DIM),
                                     logf.reshape(S, TQ, FX_HEADS))):
            dst.append(arr)
        gates_s = gates.reshape(n_s, 2 * D)

        wa, wb, wo = (w.astype(BF16) for w in (w_branch_a[l], w_branch_b[l], w_out[l]))
        g1, b1 = ln1_g[l].reshape(1, D), ln1_b[l].reshape(1, D)
        h_p = _mix_call(oa_p.reshape(n_p, -1), of_p.reshape(n_p, -1), gates_p, xp.reshape(n_p, D), wa, wb, wo, g1, b1,
                        alpha=alpha, tm=TOKEN_TILE)
        h_s = _mix_call(oa_s.reshape(n_s, -1), of_s.reshape(n_s, -1), gates_s, xs.reshape(n_s, D), wa, wb, wo, g1, b1,
                        alpha=alpha, tm=TOKEN_TILE)
        h = jnp.concatenate([h_p, h_s], axis=0)
        p_all = jnp.concatenate([p_prompt[l].reshape(n_p, -1), p_sample[l].reshape(n_s, -1)], axis=0)
        base, idx, w = _ffn_call(h, p_all, w_router[l].T.astype(BF16), router_bias[l].reshape(-1, 1),
                                 w_sh_gate[l].astype(BF16), w_sh_up[l].astype(BF16), w_sh_down[l].astype(BF16),
                                 w_ple_proj[l].astype(BF16), w_ple_gate[l].astype(BF16), alpha=alpha, tm=TOKEN_TILE)
        block_e, slot_tok, slot_w, pos = _moe_metadata(idx, w, n, EXPERT_ROWS)
        h_rows = jnp.concatenate([h, jnp.zeros((1, D), F32)], axis=0).reshape(n + 1, 1, D)
        y_sorted = _moe_call(block_e, slot_tok, slot_w, h_rows, w_exp_gate[l], w_exp_up[l], w_exp_down[l], rb=EXPERT_ROWS)
        y = _combine_call(pos, y_sorted.reshape(-1, 1, D), base, ln2_g[l].reshape(1, D), ln2_b[l].reshape(1, D),
                          tt=COMBINE_TILE)
        xp, xs = y[:n_p].reshape(B, T, D), y[n_p:].reshape(S, TQ, D)
    stack = lambda xs_: jnp.stack(xs_)
    return (xp, xs, *(stack(o) for o in outs_p), *(stack(o) for o in outs_s))
```

```python
import functools
import math

import jax
import jax.numpy as jnp
from jax import lax
from jax.experimental import pallas as pl
from jax.experimental.pallas import tpu as pltpu

F32, BF16, I32 = jnp.float32, jnp.bfloat16, jnp.int32
SDS = jax.ShapeDtypeStruct

HEAD_DIM = 64
DA_KV, DA_G = 4, 2
FX_KV, FX_G = 4, 4
FX_HEADS = FX_KV * FX_G
ROPE_THETA = 10000.0
N_EXPERTS, TOP_K, N_GROUPS, TOPK_GROUPS = 256, 8, 8, 4
GROUP_SIZE = N_EXPERTS // N_GROUPS
ROUTED_SCALE = 2.5
EPS = 1e-5
LOG2E = math.log2(math.e)
QDA_W = DA_KV * DA_G * 2 * HEAD_DIM
KDA_W = DA_KV * 2 * HEAD_DIM
VDA_W = DA_KV * 2 * HEAD_DIM
QFX_W = FX_HEADS * HEAD_DIM
KFX_W = FX_KV * HEAD_DIM
VFX_W = FX_KV * HEAD_DIM
QKV_W = QDA_W + KDA_W + VDA_W + QFX_W + KFX_W + VFX_W

LANES = 128
SUBLANES = 8
SUM_ROWS = 16
VMEM_LIMIT_BYTES = 56 * 1024 * 1024

TOKEN_TILE = 256
Q_TILE = 256
PAGES_PER_STEP = 16
EXPERT_ROWS = 256
COMBINE_TILE = 128


def _cparams(sem, **kw):
    return pltpu.CompilerParams(dimension_semantics=sem, vmem_limit_bytes=VMEM_LIMIT_BYTES, **kw)


def _full(shape):
    n = len(shape)
    return pl.BlockSpec(shape, lambda *_: (0,) * n)


def _sigmoid(x):
    return 1.0 / (1.0 + jnp.exp(-x))


def _dot(a, b):
    return jnp.dot(a, b, preferred_element_type=F32)


def _dot_nt(a, b):
    return lax.dot_general(a, b, (((1,), (1,)), ((), ())), preferred_element_type=F32)


def _layer_norm(x, g, b):
    mu = jnp.mean(x, axis=-1, keepdims=True)
    xc = x - mu
    var = jnp.mean(xc * xc, axis=-1, keepdims=True)
    return xc * lax.rsqrt(var + EPS) * g + b


def _bf16_part(x):
    return x.astype(BF16).astype(F32)


def _store_values_t(vt_ref, v, n_heads):
    tm, w = v.shape
    hd = w // n_heads
    vt = v.T.astype(BF16)
    ones = jnp.ones((SUM_ROWS, tm), BF16)
    for h in range(n_heads):
        base = h * (hd + SUM_ROWS)
        vt_ref[0, 0, base:base + hd] = vt[h * hd:(h + 1) * hd]
        vt_ref[0, 0, base + hd:base + hd + SUM_ROWS] = ones


def _proj_kernel(x_ref, cos_ref, sin_ref, wqkv_ref, wf_ref, wg_ref, bf_ref,
                 qda_ref, kda_ref, kdab_ref, vda_ref, vdat_ref, qfx_ref, kfx_ref, vfx_ref, kbfx_ref, vfxt_ref,
                 logf_ref, gates_ref, carry_ref, *, tm):
    i = pl.program_id(1)
    xb = x_ref[0].astype(BF16)
    cos, sin = cos_ref[...], sin_ref[...]
    lane = lax.broadcasted_iota(I32, (tm, LANES), 1)
    first_half = (lane % HEAD_DIM) < (HEAD_DIM // 2)
    scale = HEAD_DIM ** -0.5 * LOG2E

    def rope(y):
        partner = jnp.where(first_half, pltpu.roll(y, LANES - HEAD_DIM // 2, 1), pltpu.roll(y, HEAD_DIM // 2, 1))
        return y * cos + partner * sin

    o = 0
    y = _dot(xb, wqkv_ref[:, o:o + QDA_W])
    for c in range(QDA_W // LANES):
        sl = slice(c * LANES, (c + 1) * LANES)
        qda_ref[0, :, sl] = (rope(y[:, sl]) * scale).astype(BF16)
    o += QDA_W
    y = _dot(xb, wqkv_ref[:, o:o + KDA_W])
    for c in range(KDA_W // LANES):
        sl = slice(c * LANES, (c + 1) * LANES)
        r = rope(y[:, sl])
        kda_ref[0, :, sl] = r
        kdab_ref[0, :, sl] = r.astype(BF16)
    o += KDA_W
    y = _dot(xb, wqkv_ref[:, o:o + VDA_W])
    vda_ref[0] = y
    _store_values_t(vdat_ref, y, DA_KV)
    o += VDA_W
    y = _dot(xb, wqkv_ref[:, o:o + QFX_W])
    qfx_ref[0] = (y * scale).astype(BF16)
    o += QFX_W
    yk = _dot(xb, wqkv_ref[:, o:o + KFX_W])
    kfx_ref[0] = yk
    o += KFX_W
    yv = _dot(xb, wqkv_ref[:, o:o + VFX_W])
    vfx_ref[0] = yv
    _store_values_t(vfxt_ref, yv, FX_KV)

    f = _dot(xb, wf_ref[...]) + bf_ref[...]
    logf = jnp.minimum(f, 0.0) - jnp.log1p(jnp.exp(-jnp.abs(f)))
    logf_ref[0] = logf[:, :FX_HEADS]

    @pl.when(i == 0)
    def _():
        carry_ref[...] = jnp.zeros_like(carry_ref)

    row = lax.broadcasted_iota(I32, (tm, LANES), 0)
    cum = logf
    d = 1
    while d < tm:
        cum = cum + jnp.where(row >= d, pltpu.roll(cum, d, 0), 0.0)
        d *= 2
    cum = cum + carry_ref[0:1, :]
    carry_ref[...] = jnp.broadcast_to(cum[tm - 1:tm, :], carry_ref.shape)

    neg = -LOG2E * cum
    hi = _bf16_part(neg)
    mid = _bf16_part(neg - hi)
    lo = neg - hi - mid
    for h in range(FX_KV):
        kh = yk[:, (h // 2) * LANES:(h // 2 + 1) * LANES]
        if h % 2:
            kh = pltpu.roll(kh, HEAD_DIM, 1)
        slab = jnp.where(lane < HEAD_DIM, kh, 0.0)
        for part, (src, base) in enumerate(((hi, HEAD_DIM), (mid, HEAD_DIM + FX_G), (lo, HEAD_DIM + 2 * FX_G))):
            moved = pltpu.roll(src, base - h * FX_G, 1)
            slab = jnp.where((lane >= base) & (lane < base + FX_G), moved, slab)
        kbfx_ref[0, :, h * LANES:(h + 1) * LANES] = slab.astype(BF16)

    gates_ref[0] = _sigmoid(_dot(xb, wg_ref[...]))


def _proj_call(x, cos, sin, wqkv, wf, wg, bf, *, tm):
    B, T, D = x.shape
    nt = T // tm
    row = lambda b, i: (b, i, 0)
    tab = pl.BlockSpec((tm, LANES), lambda b, i: (i, 0))
    def rows(w, dt):
        return SDS((B, T, w), dt), pl.BlockSpec((1, tm, w), row)

    def transposed(w, n_heads):
        wt = w + n_heads * SUM_ROWS
        return SDS((B, nt, wt, tm), BF16), pl.BlockSpec((1, 1, wt, tm), lambda b, i: (b, i, 0, 0))

    outs = [rows(QDA_W, BF16), rows(KDA_W, F32), rows(KDA_W, BF16), rows(VDA_W, F32), transposed(VDA_W, DA_KV),
            rows(QFX_W, BF16), rows(KFX_W, F32), rows(VFX_W, F32), rows(FX_KV * LANES, BF16), transposed(VFX_W, FX_KV),
            rows(FX_HEADS, F32), rows(2 * D, F32)]
    out_shape = [o[0] for o in outs]
    out_specs = [o[1] for o in outs]
    return pl.pallas_call(
        functools.partial(_proj_kernel, tm=tm),
        grid=(B, nt),
        in_specs=[pl.BlockSpec((1, tm, D), row), tab, tab, _full(wqkv.shape), _full(wf.shape), _full(wg.shape),
                  _full(bf.shape)],
        out_specs=out_specs,
        out_shape=out_shape,
        scratch_shapes=[pltpu.VMEM((SUBLANES, LANES), F32)],
        compiler_params=_cparams(("arbitrary", "arbitrary")),
        name="proj",
    )(x, cos, sin, wqkv, wf, wg, bf)


def _rope_tables(pos):
    half = HEAD_DIM // 2
    inv = ROPE_THETA ** (-jnp.arange(0, HEAD_DIM, 2, dtype=F32) / HEAD_DIM)
    ang = pos.astype(F32)[:, None] * inv[None, :]
    cos, sin = jnp.cos(ang), jnp.sin(ang)
    reps = LANES // half
    cos_t = jnp.tile(cos, (1, reps))
    sin_t = jnp.concatenate([-sin, sin] * (reps // 2), axis=1)
    return cos_t, sin_t


def _attn_kernel(*refs, fox, tq, tk, lam_init):
    if fox:
        q_ref, k_ref, vt_ref, o_ref, qs_ref, sa_ref, sb_ref, m_ref, acc_ref = refs
    else:
        lam_ref, g_ref, q_ref, k_ref, vt_ref, o_ref, qs_ref, sa_ref, sb_ref, m_ref, acc_ref = refs
    qi = pl.program_id(2)
    ncol = 4 * tq
    hd = acc_ref.shape[0] - SUM_ROWS
    lane = lax.broadcasted_iota(I32, (tq, LANES), 1)
    lower = lane < HEAD_DIM

    if fox:
        qf = q_ref[0].astype(F32)
        for g in range(FX_G):
            slab = qf[:, (g // 2) * LANES:(g // 2 + 1) * LANES]
            if g % 2:
                slab = pltpu.roll(slab, HEAD_DIM, 1)
            ones = (lane == HEAD_DIM + g) | (lane == HEAD_DIM + FX_G + g) | (lane == HEAD_DIM + 2 * FX_G + g)
            qs_ref[g * tq:(g + 1) * tq, :] = jnp.where(lower, slab, jnp.where(ones, 1.0, 0.0)).astype(BF16)
    else:
        q = q_ref[0]
        zero = jnp.zeros_like(q[:, :LANES])
        for g in range(DA_G):
            slab = q[:, g * LANES:(g + 1) * LANES]
            qs_ref[(2 * g) * tq:(2 * g + 1) * tq, :] = jnp.where(lower, slab, zero)
            qs_ref[(2 * g + 1) * tq:(2 * g + 2) * tq, :] = jnp.where(lower, zero, slab)

    m_ref[...] = jnp.full_like(m_ref, -jnp.inf)
    acc_ref[...] = jnp.zeros_like(acc_ref)

    def scores(kb):
        start = pl.multiple_of(kb * tk, tk)
        return _dot_nt(k_ref[0, pl.ds(start, tk), :], qs_ref[...])

    def consume(s_ref, kb, masked):
        s = s_ref[...]
        if masked:
            key = lax.broadcasted_iota(I32, (tk, ncol), 0)
            qry = lax.broadcasted_iota(I32, (tk, ncol), 1) % tq
            s = jnp.where(key <= qry, s, -jnp.inf)
        m_prev = m_ref[...]
        m_new = jnp.maximum(m_prev, jnp.max(s, axis=0, keepdims=True))
        alpha = jnp.exp2(m_prev - m_new)
        p = jnp.exp2(s - m_new)
        acc_ref[...] = alpha * acc_ref[...] + _dot(vt_ref[0, kb], p.astype(BF16))
        m_ref[...] = m_new

    sa_ref[...] = scores(0)

    def body(t, carry):
        kb = 2 * t
        sb_ref[...] = scores(kb + 1)
        consume(sa_ref, kb, False)
        sa_ref[...] = scores(kb + 2)
        consume(sb_ref, kb + 1, False)
        return carry

    lax.fori_loop(0, qi // 2, body, 0)

    @pl.when(qi % 2 == 0)
    def _():
        consume(sa_ref, qi, True)

    @pl.when(qi % 2 == 1)
    def _():
        sb_ref[...] = scores(qi)
        consume(sa_ref, qi - 1, False)
        consume(sb_ref, qi, True)

    o = acc_ref[0:hd, :] / acc_ref[hd:hd + 1, :]
    if fox:
        for c in range(FX_G // 2):
            pair = jnp.concatenate([o[:, (2 * c) * tq:(2 * c + 1) * tq], o[:, (2 * c + 1) * tq:(2 * c + 2) * tq]], axis=0)
            o_ref[0, :, c * LANES:(c + 1) * LANES] = pair.T.astype(o_ref.dtype)
    else:
        lam = lam_ref[0]
        for g in range(DA_G):
            og = (o[:, (2 * g) * tq:(2 * g + 1) * tq] - lam * o[:, (2 * g + 1) * tq:(2 * g + 2) * tq]).T
            og = og * lax.rsqrt(jnp.mean(og * og, axis=-1, keepdims=True) + EPS) * g_ref[...] * (1.0 - lam_init)
            o_ref[0, :, g * LANES:(g + 1) * LANES] = og.astype(o_ref.dtype)


def _attn_call(q, k, vt, lam, subln, *, fox, tq, tk, lam_init):
    assert tq == tk
    B, T, _ = q.shape
    nk = T // tk
    hd = vt.shape[2] // 4
    qspec = pl.BlockSpec((1, tq, 2 * LANES), lambda b, h, i: (b, i, h))
    kspec = pl.BlockSpec((1, T, LANES), lambda b, h, i: (b, 0, h))
    vtspec = pl.BlockSpec((1, nk, hd, tk), lambda b, h, i: (b, 0, h, 0))
    if fox:
        args = (q, k, vt)
        in_specs = [qspec, kspec, vtspec]
    else:
        args = (lam, subln, q, k, vt)
        in_specs = [pl.BlockSpec(memory_space=pltpu.SMEM), _full(subln.shape), qspec, kspec, vtspec]
    return pl.pallas_call(
        functools.partial(_attn_kernel, fox=fox, tq=tq, tk=tk, lam_init=lam_init),
        grid=(B, 4, T // tq),
        in_specs=in_specs,
        out_specs=qspec,
        out_shape=SDS(q.shape, BF16),
        scratch_shapes=[pltpu.VMEM((4 * tq, LANES), BF16), pltpu.VMEM((tk, 4 * tq), F32), pltpu.VMEM((tk, 4 * tq), F32),
                        pltpu.VMEM((1, 4 * tq), F32), pltpu.VMEM((hd, 4 * tq), F32)],
        compiler_params=_cparams(("arbitrary", "arbitrary", "arbitrary")),
        name="attn_fox" if fox else "attn_da",
    )(*args)


def _lane_cumsum(x):
    lane = lax.broadcasted_iota(I32, x.shape, 1)
    d = 1
    while d < x.shape[1]:
        x = x + jnp.where(lane >= d, pltpu.roll(x, d, 1), 0.0)
        d *= 2
    return x


def _sample_kernel(pt_ref, lam_ref, g_ref, qda_ref, qfx_ref, kdan_ref, vdan_ref, kfxn_ref, vfxn_ref, lfn_ref, *rest,
                   npg, nsteps, tq, lam_init):
    kda = rest[0 * npg:1 * npg]
    vda = rest[1 * npg:2 * npg]
    kfx = rest[2 * npg:3 * npg]
    vfx = rest[3 * npg:4 * npg]
    lfp = rest[4 * npg:5 * npg]
    (oa_ref, of_ref, qbda_ref, qbfx_ref, mda_ref, lda_ref, accda_ref, mfx_ref, lfx_ref, accfx_ref,
     carry_ref, pad_ref) = rest[5 * npg:]
    j = pl.program_id(1)
    nr = LANES
    lane8 = lax.broadcasted_iota(I32, (tq, LANES), 1)
    rows_per_head_da = nr // DA_KV

    @pl.when(j == 0)
    def _():
        qbda_ref[...] = jnp.zeros_like(qbda_ref)
        qbfx_ref[...] = jnp.zeros_like(qbfx_ref)
        for h in range(DA_KV):
            for g in range(DA_G):
                slab = qda_ref[:, (h * DA_G + g) * LANES:(h * DA_G + g + 1) * LANES]
                for m in range(2):
                    rb = (h * DA_G + g) * 2 + m
                    qbda_ref[rb * tq:(rb + 1) * tq, h * LANES:(h + 1) * LANES] = jnp.where(
                        (lane8 // HEAD_DIM) == m, slab, 0.0)
        for h in range(FX_KV):
            for g in range(FX_G):
                head = h * FX_G + g
                slab = qfx_ref[:, (head // 2) * LANES:(head // 2 + 1) * LANES]
                if (head % 2) != (h % 2):
                    slab = pltpu.roll(slab, HEAD_DIM, 1)
                qbfx_ref[head * tq:(head + 1) * tq, (h // 2) * LANES:(h // 2 + 1) * LANES] = jnp.where(
                    (lane8 // HEAD_DIM) == (h % 2), slab, 0.0)
        for r in (mda_ref, mfx_ref):
            r[...] = jnp.full_like(r, -jnp.inf)
        for r in (lda_ref, lfx_ref, accda_ref, accfx_ref, carry_ref):
            r[...] = jnp.zeros_like(r)

    qbda = qbda_ref[...].astype(BF16)
    qbfx = qbfx_ref[...].astype(BF16)

    def softmax_update(s, m_ref, l_ref):
        m_prev = m_ref[...]
        m_new = jnp.maximum(m_prev, jnp.max(s, axis=1, keepdims=True))
        alpha = jnp.exp2(m_prev - m_new)
        p = jnp.exp2(s - m_new)
        l_ref[...] = alpha * l_ref[...] + jnp.sum(p, axis=1, keepdims=True)
        m_ref[...] = m_new
        return alpha, p.astype(BF16)

    def fox_bias(s, negc):
        return jnp.concatenate([s[hd * tq:(hd + 1) * tq] + negc[hd:hd + 1, :] for hd in range(FX_HEADS)], axis=0)

    s_da = jnp.concatenate([_dot(qbda, kda[p][...].astype(BF16)) for p in range(npg)], axis=1)
    alpha, pr = softmax_update(s_da, mda_ref, lda_ref)
    for h in range(DA_KV):
        rs = slice(h * rows_per_head_da, (h + 1) * rows_per_head_da)
        pv = 0.0
        for p in range(npg):
            vh = vda[p][pl.ds(h, LANES, stride=DA_KV), :].astype(BF16)
            pv = pv + _dot(pr[rs, p * LANES:(p + 1) * LANES], vh)
        accda_ref[rs, :] = alpha[rs] * accda_ref[rs, :] + pv

    s_list = []
    for p in range(npg):
        cum = _lane_cumsum(lfp[p][...]) + carry_ref[...]
        carry_ref[...] = jnp.broadcast_to(cum[:, LANES - 1:LANES], carry_ref.shape)
        s_list.append(fox_bias(_dot(qbfx, kfx[p][...].astype(BF16)), -LOG2E * cum))
    alpha, pr = softmax_update(jnp.concatenate(s_list, axis=1), mfx_ref, lfx_ref)
    pv = 0.0
    for p in range(npg):
        pv = pv + _dot_nt(pr[:, p * LANES:(p + 1) * LANES], vfx[p][...].astype(BF16))
    accfx_ref[...] = alpha * accfx_ref[...] + pv

    @pl.when(j == nsteps - 1)
    def _():
        rowq = lax.broadcasted_iota(I32, (nr, LANES), 0) % tq
        col = lax.broadcasted_iota(I32, (nr, LANES), 1)
        valid = (col < tq) & (col <= rowq)

        def padded(x):
            w = x.shape[1]
            wp = max(w, LANES)
            pad_ref[:, :wp] = jnp.zeros((LANES, wp), F32)
            pad_ref[0:tq, :w] = x
            return pad_ref[:, :wp]

        s = jnp.where(valid, _dot_nt(qbda, padded(kdan_ref[...]).astype(BF16)), -jnp.inf)
        alpha, pr = softmax_update(s, mda_ref, lda_ref)
        vpad = padded(vdan_ref[...]).astype(BF16)
        for h in range(DA_KV):
            rs = slice(h * rows_per_head_da, (h + 1) * rows_per_head_da)
            accda_ref[rs, :] = alpha[rs] * accda_ref[rs, :] + _dot(pr[rs], vpad[:, h * LANES:(h + 1) * LANES])

        lft = padded(lfn_ref[...]).T
        cum = _lane_cumsum(lft[:FX_HEADS]) + carry_ref[...]
        s = fox_bias(_dot_nt(qbfx, padded(kfxn_ref[...]).astype(BF16)), -LOG2E * cum)
        s = jnp.where(valid, s, -jnp.inf)
        alpha, pr = softmax_update(s, mfx_ref, lfx_ref)
        accfx_ref[...] = alpha * accfx_ref[...] + _dot(pr, padded(vfxn_ref[...]).astype(BF16))

        lam = lam_ref[0]
        oda = accda_ref[...] / lda_ref[...]
        for h in range(DA_KV):
            for g in range(DA_G):
                rb = (h * DA_G + g) * 2
                og = oda[rb * tq:(rb + 1) * tq] - lam * oda[(rb + 1) * tq:(rb + 2) * tq]
                og = og * lax.rsqrt(jnp.mean(og * og, axis=-1, keepdims=True) + EPS) * g_ref[...] * (1.0 - lam_init)
                oa_ref[:, (h * DA_G + g) * LANES:(h * DA_G + g + 1) * LANES] = og
        ofx = accfx_ref[...] / lfx_ref[...]
        for sl in range(FX_HEADS // 2):
            pieces = []
            for head in (2 * sl, 2 * sl + 1):
                h = head // FX_G
                piece = ofx[head * tq:(head + 1) * tq, (h // 2) * LANES:(h // 2 + 1) * LANES]
                if (head % 2) != (h % 2):
                    piece = pltpu.roll(piece, HEAD_DIM, 1)
                pieces.append(piece)
            of_ref[:, sl * LANES:(sl + 1) * LANES] = jnp.where(lane8 < HEAD_DIM, pieces[0], pieces[1])


def _sample_call(page_table, lam, subln, qda, qfx, kdan, vdan, kfxn, vfxn, lfn, kda_c, vda_c, kfx_c, vfx_c, lf_c,
                 *, npg, lam_init):
    S, tq, _ = qda.shape
    n_pages = page_table.shape[1]
    nsteps = n_pages // npg
    assert nsteps * npg == n_pages and FX_HEADS * tq == LANES

    def new(w):
        return pl.BlockSpec((None, tq, w), lambda s, j, pt: (s, 0, 0))

    def page(rows, p):
        return pl.BlockSpec((None, rows, LANES), lambda s, j, pt: (pt[s, j * npg + p], 0, 0))

    in_specs = [pl.BlockSpec(memory_space=pltpu.SMEM), pl.BlockSpec(subln.shape, lambda s, j, pt: (0, 0)),
                new(QDA_W), new(QFX_W), new(KDA_W), new(VDA_W), new(KFX_W), new(VFX_W), new(FX_HEADS)]
    args = [lam, subln, qda, qfx, kdan, vdan, kfxn, vfxn, lfn]
    for arr in (kda_c, vda_c, kfx_c, vfx_c, lf_c):
        for p in range(npg):
            in_specs.append(page(arr.shape[1], p))
            args.append(arr)
    out_spec = pl.BlockSpec((None, tq, QDA_W), lambda s, j, pt: (s, 0, 0))
    grid_spec = pltpu.PrefetchScalarGridSpec(
        num_scalar_prefetch=1,
        grid=(S, nsteps),
        in_specs=in_specs,
        out_specs=[out_spec, out_spec],
        scratch_shapes=[pltpu.VMEM((LANES, KDA_W), F32), pltpu.VMEM((LANES, KFX_W), F32),
                        pltpu.VMEM((LANES, 1), F32), pltpu.VMEM((LANES, 1), F32), pltpu.VMEM((LANES, LANES), F32),
                        pltpu.VMEM((LANES, 1), F32), pltpu.VMEM((LANES, 1), F32), pltpu.VMEM((LANES, KFX_W), F32),
                        pltpu.VMEM((FX_HEADS, LANES), F32), pltpu.VMEM((LANES, KDA_W), F32)],
    )
    return pl.pallas_call(
        functools.partial(_sample_kernel, npg=npg, nsteps=nsteps, tq=tq, lam_init=lam_init),
        grid_spec=grid_spec,
        out_shape=[SDS((S, tq, QDA_W), F32), SDS((S, tq, QFX_W), F32)],
        compiler_params=_cparams(("arbitrary", "arbitrary")),
        name="sample_attn",
    )(page_table, *args)


def _mix_kernel(oa_ref, of_ref, gates_ref, x_ref, wa_ref, wb_ref, wo_ref, g_ref, b_ref, h_ref, *, alpha):
    d = x_ref.shape[-1]
    a = _dot(oa_ref[...].astype(BF16), wa_ref[...])
    b = _dot(of_ref[...].astype(BF16), wb_ref[...])
    merged = gates_ref[:, :d] * a + gates_ref[:, d:] * b
    mix = _dot(merged.astype(BF16), wo_ref[...])
    h_ref[...] = _layer_norm(alpha * x_ref[...] + mix, g_ref[...], b_ref[...])


def _mix_call(oa, of, gates, x, wa, wb, wo, g, b, *, alpha, tm):
    n, d = x.shape
    row = lambda i: (i, 0)
    return pl.pallas_call(
        functools.partial(_mix_kernel, alpha=alpha),
        grid=(n // tm,),
        in_specs=[pl.BlockSpec((tm, oa.shape[1]), row), pl.BlockSpec((tm, of.shape[1]), row),
                  pl.BlockSpec((tm, 2 * d), row), pl.BlockSpec((tm, d), row),
                  _full(wa.shape), _full(wb.shape), _full(wo.shape), _full(g.shape), _full(b.shape)],
        out_specs=pl.BlockSpec((tm, d), row),
        out_shape=SDS((n, d), F32),
        compiler_params=_cparams(("arbitrary",)),
        name="mix_ln1",
    )(oa, of, gates, x, wa, wb, wo, g, b)


def _route(scores, biased):
    n_exp, tm = scores.shape
    rid = lax.broadcasted_iota(I32, (n_exp, tm), 0)
    gr = lax.broadcasted_iota(I32, (GROUP_SIZE, tm), 0)
    neg = -jnp.inf
    gscore = []
    for g in range(N_GROUPS):
        v = biased[g * GROUP_SIZE:(g + 1) * GROUP_SIZE]
        m1 = jnp.max(v, axis=0, keepdims=True)
        i1 = jnp.min(jnp.where(v == m1, gr, GROUP_SIZE), axis=0, keepdims=True)
        m2 = jnp.max(jnp.where(gr == i1, neg, v), axis=0, keepdims=True)
        gscore.append(m1 + m2)
    chosen = [jnp.zeros((1, tm), I32) for _ in range(N_GROUPS)]
    for _ in range(TOPK_GROUPS):
        m = functools.reduce(jnp.maximum, gscore)
        first = functools.reduce(jnp.minimum, [jnp.where(gscore[g] == m, g, N_GROUPS) for g in range(N_GROUPS)])
        for g in range(N_GROUPS):
            hit = first == g
            chosen[g] = jnp.where(hit, 1, chosen[g])
            gscore[g] = jnp.where(hit, neg, gscore[g])
    cur = jnp.concatenate(
        [jnp.where(chosen[g] > 0, biased[g * GROUP_SIZE:(g + 1) * GROUP_SIZE], neg) for g in range(N_GROUPS)], axis=0)
    kid = lax.broadcasted_iota(I32, (TOP_K, tm), 0)
    idx = jnp.zeros((TOP_K, tm), I32)
    w = jnp.zeros((TOP_K, tm), F32)
    for k in range(TOP_K):
        m = jnp.max(cur, axis=0, keepdims=True)
        i = jnp.min(jnp.where(cur == m, rid, n_exp), axis=0, keepdims=True)
        hit = rid == i
        wk = jnp.sum(jnp.where(hit, scores, 0.0), axis=0, keepdims=True)
        idx = jnp.where(kid == k, i, idx)
        w = jnp.where(kid == k, wk, w)
        cur = jnp.where(hit, neg, cur)
    w = w / jnp.sum(w, axis=0, keepdims=True) * ROUTED_SCALE
    return idx, w


def _ffn_kernel(h_ref, p_ref, wrt_ref, rb_ref, wsg_ref, wsu_ref, wsd_ref, wpp_ref, wpg_ref,
                base_ref, idx_ref, w_ref, hrows_ref, *, alpha):
    h = h_ref[...]
    hrows_ref[...] = h.reshape(hrows_ref.shape)
    hb = h.astype(BF16)
    scores = _sigmoid(_dot_nt(wrt_ref[...], hb))
    idx, w = _route(scores, scores + rb_ref[...])
    idx_ref[...] = idx
    w_ref[...] = w
    gate = _dot(hb, wsg_ref[...])
    shared = _dot((gate * _sigmoid(gate) * _dot(hb, wsu_ref[...])).astype(BF16), wsd_ref[...])
    ple = _sigmoid(_dot(hb, wpg_ref[...])) * _dot(p_ref[...].astype(BF16), wpp_ref[...])
    base_ref[...] = alpha * h + shared + ple


def _ffn_call(h, p, wrt, rbias, wsg, wsu, wsd, wpp, wpg, *, alpha, tm):
    n, d = h.shape
    row = lambda i: (i, 0)
    col = lambda i: (0, i)
    return pl.pallas_call(
        functools.partial(_ffn_kernel, alpha=alpha),
        grid=(n // tm,),
        in_specs=[pl.BlockSpec((tm, d), row), pl.BlockSpec((tm, p.shape[1]), row), _full(wrt.shape), _full(rbias.shape),
                  _full(wsg.shape), _full(wsu.shape), _full(wsd.shape), _full(wpp.shape), _full(wpg.shape)],
        out_specs=[pl.BlockSpec((tm, d), row), pl.BlockSpec((TOP_K, tm), col), pl.BlockSpec((TOP_K, tm), col),
                   pl.BlockSpec((tm, 1, d), lambda i: (i, 0, 0))],
        out_shape=[SDS((n, d), F32), SDS((TOP_K, n), I32), SDS((TOP_K, n), F32), SDS((n, 1, d), F32)],
        compiler_params=_cparams(("arbitrary",)),
        name="router_shared_ple",
    )(h, p, wrt, rbias, wsg, wsu, wsd, wpp, wpg)


def _gather_rows(idx_ref, n, src_hbm, dst, sem):
    def body(r, carry):
        pltpu.make_async_copy(src_hbm.at[idx_ref[0, r]], dst.at[r], sem).start()
        return carry
    lax.fori_loop(0, n, body, 0, unroll=8)


def _wait_rows(n, src_hbm, dst, sem):
    pltpu.make_async_copy(src_hbm.at[pl.ds(0, n)], dst, sem).wait()


def _moe_kernel(be_ref, on_ref, tokc_ref, tokn_ref, sw_ref, h_hbm, wg_ref, wu_ref, wd_ref, o_ref,
                xbuf, x2d, wgb, wub, wdb, sem, *, rb, nb):
    i = pl.program_id(0)
    slot = i % 2
    nxt = jnp.minimum(i + 1, nb - 1)

    @pl.when((i == 0) & (on_ref[0] > 0))
    def _():
        _gather_rows(tokc_ref, rb, h_hbm, xbuf.at[0], sem.at[0])

    @pl.when((i + 1 < nb) & (on_ref[nxt] > 0))
    def _():
        _gather_rows(tokn_ref, rb, h_hbm, xbuf.at[1 - slot], sem.at[1 - slot])

    @pl.when(on_ref[i] > 0)
    def _():
        @pl.when((i == 0) | (be_ref[i] != be_ref[jnp.maximum(i - 1, 0)]))
        def _():
            wgb[...] = wg_ref[...].astype(BF16)
            wub[...] = wu_ref[...].astype(BF16)
            wdb[...] = wd_ref[...].astype(BF16)

        _wait_rows(rb, h_hbm, xbuf.at[slot], sem.at[slot])
        x2d[...] = xbuf[slot].reshape(x2d.shape)
        x = x2d[...].astype(BF16)
        gate = _dot(x, wgb[...])
        hid = gate * _sigmoid(gate) * _dot(x, wub[...])
        y = _dot(hid.astype(BF16), wdb[...]) * sw_ref[...]
        o_ref[...] = y.reshape(o_ref.shape)

    @pl.when(on_ref[i] == 0)
    def _():
        o_ref[...] = jnp.zeros_like(o_ref)


def _moe_call(block_e, block_on, slot_tok, slot_w, h_rows, wg, wu, wd, *, rb):
    nb = block_e.shape[0]
    d = h_rows.shape[-1]
    de = wg.shape[-1]
    tok3 = slot_tok.reshape(nb, 1, rb)
    sw3 = slot_w.reshape(nb, rb, 1)
    grid_spec = pltpu.PrefetchScalarGridSpec(
        num_scalar_prefetch=2,
        grid=(nb,),
        in_specs=[pl.BlockSpec((None, 1, rb), lambda i, be, on: (i, 0, 0), memory_space=pltpu.SMEM),
                  pl.BlockSpec((None, 1, rb), lambda i, be, on: (jnp.minimum(i + 1, nb - 1), 0, 0),
                               memory_space=pltpu.SMEM),
                  pl.BlockSpec((None, rb, 1), lambda i, be, on: (i, 0, 0)),
                  pl.BlockSpec(memory_space=pl.ANY),
                  pl.BlockSpec((None, d, de), lambda i, be, on: (be[i], 0, 0)),
                  pl.BlockSpec((None, d, de), lambda i, be, on: (be[i], 0, 0)),
                  pl.BlockSpec((None, de, d), lambda i, be, on: (be[i], 0, 0))],
        out_specs=pl.BlockSpec((rb, 1, d), lambda i, be, on: (i, 0, 0)),
        scratch_shapes=[pltpu.VMEM((2, rb, 1, d), F32), pltpu.VMEM((rb, d), F32), pltpu.VMEM((d, de), BF16),
                        pltpu.VMEM((d, de), BF16),
                        pltpu.VMEM((de, d), BF16), pltpu.SemaphoreType.DMA((2,))],
    )
    return pl.pallas_call(
        functools.partial(_moe_kernel, rb=rb, nb=nb),
        grid_spec=grid_spec,
        out_shape=SDS((nb * rb, 1, d), F32),
        compiler_params=_cparams(("arbitrary",), disable_bounds_checks=True),
        name="moe_experts",
    )(block_e, block_on, tok3, tok3, sw3, h_rows, wg, wu, wd)


def _moe_metadata(idx, w, n, rb):
    a = TOP_K * n
    assert a % rb == 0
    n_dummy = N_EXPERTS * rb
    flat_e = idx.reshape(-1)
    experts = jnp.arange(N_EXPERTS, dtype=I32)
    counts = jnp.sum((flat_e[:, None] == experts[None, :]).astype(I32), axis=0)
    n_fill = (-counts) % rb
    fill_key = jnp.where(jnp.arange(rb, dtype=I32)[None, :] < n_fill[:, None], 2 * experts[:, None] + 1, 2 * N_EXPERTS)
    keys = jnp.concatenate([2 * flat_e, fill_key.reshape(-1)])
    assign = jnp.concatenate([jnp.arange(a, dtype=I32), jnp.full((n_dummy,), a, I32)])
    weight = jnp.concatenate([w.reshape(-1), jnp.zeros((n_dummy,), F32)])
    keys_s, assign_s, slot_w = lax.sort((keys, assign, weight), num_keys=1)
    slot_tok = jnp.where(assign_s < a, assign_s % n, 0)
    block_key = keys_s[::rb]
    block_e = jnp.minimum(block_key // 2, N_EXPERTS - 1)
    block_on = (block_key < 2 * N_EXPERTS).astype(I32)
    _, slot_of = lax.sort((assign_s, jnp.arange(a + n_dummy, dtype=I32)), num_keys=1)
    return block_e, block_on, slot_tok, slot_w, slot_of[:a].reshape(TOP_K, n)


def _combine_kernel(posc_ref, posn_ref, y_hbm, base_ref, g_ref, b_ref, o_ref, buf, acc2d, sem, *, tt, nt):
    i = pl.program_id(0)
    slot = i % 2
    nrows = TOP_K * tt

    @pl.when(i == 0)
    def _():
        _gather_rows(posc_ref, nrows, y_hbm, buf.at[0], sem.at[0])

    @pl.when(i + 1 < nt)
    def _():
        _gather_rows(posn_ref, nrows, y_hbm, buf.at[1 - slot], sem.at[1 - slot])

    _wait_rows(nrows, y_hbm, buf.at[slot], sem.at[slot])
    tot = buf[slot, 0:tt]
    for k in range(1, TOP_K):
        tot = tot + buf[slot, k * tt:(k + 1) * tt]
    acc2d[...] = tot.reshape(tt, acc2d.shape[-1])
    o_ref[...] = _layer_norm(base_ref[...] + acc2d[...], g_ref[...], b_ref[...])


def _combine_call(pos, y_rows, base, g, b, *, tt):
    n, d = base.shape
    nt = n // tt
    pos3 = pos.reshape(TOP_K, nt, tt).transpose(1, 0, 2).reshape(nt, 1, TOP_K * tt)
    row = lambda i: (i, 0)
    return pl.pallas_call(
        functools.partial(_combine_kernel, tt=tt, nt=nt),
        grid=(nt,),
        in_specs=[pl.BlockSpec((None, 1, TOP_K * tt), lambda i: (i, 0, 0), memory_space=pltpu.SMEM),
                  pl.BlockSpec((None, 1, TOP_K * tt), lambda i: (jnp.minimum(i + 1, nt - 1), 0, 0),
                               memory_space=pltpu.SMEM),
                  pl.BlockSpec(memory_space=pl.ANY),
                  pl.BlockSpec((tt, d), row), _full(g.shape), _full(b.shape)],
        out_specs=pl.BlockSpec((tt, d), row),
        out_shape=SDS((n, d), F32),
        scratch_shapes=[pltpu.VMEM((2, TOP_K * tt, 1, d), F32), pltpu.VMEM((tt, d), F32), pltpu.SemaphoreType.DMA((2,))],
        compiler_params=_cparams(("arbitrary",), disable_bounds_checks=True),
        name="combine_ln2",
    )(pos3, pos3, y_rows, base, g, b)


def kernel(x_prompt, x_sample, cache_da_k, cache_da_v, cache_fox_k, cache_fox_v, cache_fox_logf, page_table, p_prompt, p_sample, w_in, b_forget, lambda_q1, lambda_k1, lambda_q2, lambda_k2, subln_g, w_branch_a, w_branch_b, w_out, ln1_g, ln1_b, w_router, router_bias, w_exp_gate, w_exp_up, w_exp_down, w_sh_gate, w_sh_up, w_sh_down, w_ple_proj, w_ple_gate, ln2_g, ln2_b):
    depth = w_in.shape[0]
    alpha = (2 * depth) ** 0.25
    B, T, D = x_prompt.shape
    S, TQ, _ = x_sample.shape
    n_pool, page_size = cache_da_k.shape[1], cache_da_k.shape[2]
    past_len = page_table.shape[1] * page_size
    n_p, n_s = B * T, S * TQ
    n = n_p + n_s
    cos_p, sin_p = _rope_tables(jnp.arange(T))
    cos_s, sin_s = _rope_tables(jnp.tile(past_len + jnp.arange(TQ), S))
    xp, xs = x_prompt, x_sample
    outs_p, outs_s = [[] for _ in range(5)], [[] for _ in range(5)]
    for l in range(depth):
        lam_init = 0.8 - 0.6 * math.exp(-0.3 * l)
        lam = (jnp.exp(jnp.sum(lambda_q1[l] * lambda_k1[l])) - jnp.exp(jnp.sum(lambda_q2[l] * lambda_k2[l]))
               + lam_init).reshape(1).astype(F32)
        subln = subln_g[l].reshape(1, -1)
        wl = w_in[l]
        wqkv = wl[:, :QKV_W].astype(BF16)
        wf = jnp.pad(wl[:, QKV_W:QKV_W + FX_HEADS], ((0, 0), (0, LANES - FX_HEADS))).astype(BF16)
        wg = wl[:, QKV_W + FX_HEADS:].astype(BF16)
        bf = jnp.pad(b_forget[l], (0, LANES - FX_HEADS)).reshape(1, LANES)

        (qda, kda, kdab, vda, vdat, qfx, kfx, vfx, kbfx, vfxt, logf, gates) = _proj_call(
            xp, cos_p, sin_p, wqkv, wf, wg, bf, tm=TOKEN_TILE)
        oa_p = _attn_call(qda, kdab, vdat, lam, subln, fox=False, tq=Q_TILE, tk=TOKEN_TILE, lam_init=lam_init)
        of_p = _attn_call(qfx, kbfx, vfxt, None, None, fox=True, tq=Q_TILE, tk=TOKEN_TILE, lam_init=lam_init)
        for dst, arr in zip(outs_p, (kda.reshape(B, T, DA_KV, 2, HEAD_DIM), vda.reshape(B, T, DA_KV, 2 * HEAD_DIM),
                                     kfx.reshape(B, T, FX_KV, HEAD_DIM), vfx.reshape(B, T, FX_KV, HEAD_DIM), logf)):
            dst.append(arr)
        gates_p = gates.reshape(n_p, 2 * D)

        (qda, kda, _, vda, _, qfx, kfx, vfx, _, _, logf, gates) = _proj_call(
            xs.reshape(1, n_s, D), cos_s, sin_s, wqkv, wf, wg, bf, tm=TOKEN_TILE)
        per_seq = lambda a: a.reshape(S, TQ, a.shape[-1])
        kT_da = jnp.transpose(cache_da_k[l], (0, 2, 3, 4, 1)).reshape(n_pool, KDA_W, page_size)
        v_da = cache_da_v[l].reshape(n_pool, page_size * DA_KV, 2 * HEAD_DIM)
        kT_fx = jnp.transpose(cache_fox_k[l], (0, 2, 3, 1)).reshape(n_pool, KFX_W, page_size)
        vT_fx = jnp.transpose(cache_fox_v[l], (0, 2, 3, 1)).reshape(n_pool, VFX_W, page_size)
        lfT = jnp.transpose(cache_fox_logf[l], (0, 2, 1))
        oa_s, of_s = _sample_call(page_table, lam, subln, per_seq(qda).astype(F32), per_seq(qfx).astype(F32),
                                  per_seq(kda), per_seq(vda), per_seq(kfx), per_seq(vfx), per_seq(logf),
                                  kT_da, v_da, kT_fx, vT_fx, lfT, npg=PAGES_PER_STEP, lam_init=lam_init)
        for dst, arr in zip(outs_s, (kda.reshape(S, TQ, DA_KV, 2, HEAD_DIM), vda.reshape(S, TQ, DA_KV, 2 * HEAD_DIM),
                                     kfx.reshape(S, TQ, FX_KV, HEAD_DIM), vfx.reshape(S, TQ, FX_KV, HEAD_DIM),
                                     logf.reshape(S, TQ, FX_HEADS))):
            dst.append(arr)
        gates_s = gates.reshape(n_s, 2 * D)

        wa, wb, wo = (w.astype(BF16) for w in (w_branch_a[l], w_branch_b[l], w_out[l]))
        g1, b1 = ln1_g[l].reshape(1, D), ln1_b[l].reshape(1, D)
        h_p = _mix_call(oa_p.reshape(n_p, -1), of_p.reshape(n_p, -1), gates_p, xp.reshape(n_p, D), wa, wb, wo, g1, b1,
                        alpha=alpha, tm=TOKEN_TILE)
        h_s = _mix_call(oa_s.reshape(n_s, -1), of_s.reshape(n_s, -1), gates_s, xs.reshape(n_s, D), wa, wb, wo, g1, b1,
                        alpha=alpha, tm=TOKEN_TILE)
        h = jnp.concatenate([h_p, h_s], axis=0)
        p_all = jnp.concatenate([p_prompt[l].reshape(n_p, -1), p_sample[l].reshape(n_s, -1)], axis=0)
        base, idx, w, h_rows = _ffn_call(
            h, p_all, w_router[l].T.astype(BF16), router_bias[l].reshape(-1, 1), w_sh_gate[l].astype(BF16),
            w_sh_up[l].astype(BF16), w_sh_down[l].astype(BF16), w_ple_proj[l].astype(BF16), w_ple_gate[l].astype(BF16),
            alpha=alpha, tm=TOKEN_TILE)
        block_e, block_on, slot_tok, slot_w, slot_of = _moe_metadata(idx, w, n, EXPERT_ROWS)
        y_rows = _moe_call(block_e, block_on, slot_tok, slot_w, h_rows, w_exp_gate[l], w_exp_up[l], w_exp_down[l],
                           rb=EXPERT_ROWS)
        y = _combine_call(slot_of, y_rows, base, ln2_g[l].reshape(1, D), ln2_b[l].reshape(1, D), tt=COMBINE_TILE)
        xp, xs = y[:n_p].reshape(B, T, D), y[n_p:].reshape(S, TQ, D)
    stack = lambda xs_: jnp.stack(xs_)
    return (xp, xs, *(stack(o) for o in outs_p), *(stack(o) for o in outs_s))
```

```python
import functools
import math

import jax
import jax.numpy as jnp
from jax import lax
from jax.experimental import pallas as pl
from jax.experimental.pallas import tpu as pltpu

F32, BF16, I32 = jnp.float32, jnp.bfloat16, jnp.int32
SDS = jax.ShapeDtypeStruct

HEAD_DIM = 64
DA_KV, DA_G = 4, 2
FX_KV, FX_G = 4, 4
FX_HEADS = FX_KV * FX_G
ROPE_THETA = 10000.0
N_EXPERTS, TOP_K, N_GROUPS, TOPK_GROUPS = 256, 8, 8, 4
GROUP_SIZE = N_EXPERTS // N_GROUPS
ROUTED_SCALE = 2.5
EPS = 1e-5
LOG2E = math.log2(math.e)
QDA_W = DA_KV * DA_G * 2 * HEAD_DIM
KDA_W = DA_KV * 2 * HEAD_DIM
VDA_W = DA_KV * 2 * HEAD_DIM
QFX_W = FX_HEADS * HEAD_DIM
KFX_W = FX_KV * HEAD_DIM
VFX_W = FX_KV * HEAD_DIM
QKV_W = QDA_W + KDA_W + VDA_W + QFX_W + KFX_W + VFX_W

LANES = 128
SUBLANES = 8
SUM_ROWS = 16
VMEM_LIMIT_BYTES = 56 * 1024 * 1024

TOKEN_TILE = 256
Q_TILE = 512
PAGES_PER_STEP = 16
EXPERT_ROWS = 256
COMBINE_TILE = 128


def _cparams(sem, **kw):
    return pltpu.CompilerParams(dimension_semantics=sem, vmem_limit_bytes=VMEM_LIMIT_BYTES, **kw)


def _full(shape):
    n = len(shape)
    return pl.BlockSpec(shape, lambda *_: (0,) * n)


def _sigmoid(x):
    return 1.0 / (1.0 + jnp.exp(-x))


def _dot(a, b):
    return jnp.dot(a, b, preferred_element_type=F32)


def _dot_nt(a, b):
    return lax.dot_general(a, b, (((1,), (1,)), ((), ())), preferred_element_type=F32)


def _layer_norm(x, g, b):
    mu = jnp.mean(x, axis=-1, keepdims=True)
    xc = x - mu
    var = jnp.mean(xc * xc, axis=-1, keepdims=True)
    return xc * lax.rsqrt(var + EPS) * g + b


def _bf16_part(x):
    return x.astype(BF16).astype(F32)


def _store_values_t(vt_ref, v, n_heads):
    tm, w = v.shape
    hd = w // n_heads
    vt = v.T.astype(BF16)
    ones = jnp.ones((SUM_ROWS, tm), BF16)
    for h in range(n_heads):
        base = h * (hd + SUM_ROWS)
        vt_ref[0, 0, base:base + hd] = vt[h * hd:(h + 1) * hd]
        vt_ref[0, 0, base + hd:base + hd + SUM_ROWS] = ones


def _proj_kernel(x_ref, cos_ref, sin_ref, wqkv_ref, wf_ref, wg_ref, bf_ref,
                 qda_ref, kda_ref, kdab_ref, vda_ref, vdat_ref, qfx_ref, kfx_ref, vfx_ref, kbfx_ref, vfxt_ref,
                 logf_ref, gates_ref, carry_ref, *, tm):
    i = pl.program_id(1)
    xb = x_ref[0].astype(BF16)
    cos, sin = cos_ref[...], sin_ref[...]
    lane = lax.broadcasted_iota(I32, (tm, LANES), 1)
    first_half = (lane % HEAD_DIM) < (HEAD_DIM // 2)
    scale = HEAD_DIM ** -0.5 * LOG2E

    def rope(y):
        partner = jnp.where(first_half, pltpu.roll(y, LANES - HEAD_DIM // 2, 1), pltpu.roll(y, HEAD_DIM // 2, 1))
        return y * cos + partner * sin

    o = 0
    y = _dot(xb, wqkv_ref[:, o:o + QDA_W])
    for c in range(QDA_W // LANES):
        sl = slice(c * LANES, (c + 1) * LANES)
        qda_ref[0, :, sl] = (rope(y[:, sl]) * scale).astype(BF16)
    o += QDA_W
    y = _dot(xb, wqkv_ref[:, o:o + KDA_W])
    for c in range(KDA_W // LANES):
        sl = slice(c * LANES, (c + 1) * LANES)
        r = rope(y[:, sl])
        kda_ref[0, :, sl] = r
        kdab_ref[0, :, sl] = r.astype(BF16)
    o += KDA_W
    y = _dot(xb, wqkv_ref[:, o:o + VDA_W])
    vda_ref[0] = y
    _store_values_t(vdat_ref, y, DA_KV)
    o += VDA_W
    y = _dot(xb, wqkv_ref[:, o:o + QFX_W])
    qfx_ref[0] = (y * scale).astype(BF16)
    o += QFX_W
    yk = _dot(xb, wqkv_ref[:, o:o + KFX_W])
    kfx_ref[0] = yk
    o += KFX_W
    yv = _dot(xb, wqkv_ref[:, o:o + VFX_W])
    vfx_ref[0] = yv
    _store_values_t(vfxt_ref, yv, FX_KV)

    f = _dot(xb, wf_ref[...]) + bf_ref[...]
    logf = jnp.minimum(f, 0.0) - jnp.log1p(jnp.exp(-jnp.abs(f)))
    logf_ref[0] = logf[:, :FX_HEADS]

    @pl.when(i == 0)
    def _():
        carry_ref[...] = jnp.zeros_like(carry_ref)

    row = lax.broadcasted_iota(I32, (tm, LANES), 0)
    cum = logf
    d = 1
    while d < tm:
        cum = cum + jnp.where(row >= d, pltpu.roll(cum, d, 0), 0.0)
        d *= 2
    cum = cum + carry_ref[0:1, :]
    carry_ref[...] = jnp.broadcast_to(cum[tm - 1:tm, :], carry_ref.shape)

    neg = -LOG2E * cum
    hi = _bf16_part(neg)
    mid = _bf16_part(neg - hi)
    lo = neg - hi - mid
    for h in range(FX_KV):
        kh = yk[:, (h // 2) * LANES:(h // 2 + 1) * LANES]
        if h % 2:
            kh = pltpu.roll(kh, HEAD_DIM, 1)
        slab = jnp.where(lane < HEAD_DIM, kh, 0.0)
        for part, (src, base) in enumerate(((hi, HEAD_DIM), (mid, HEAD_DIM + FX_G), (lo, HEAD_DIM + 2 * FX_G))):
            moved = pltpu.roll(src, base - h * FX_G, 1)
            slab = jnp.where((lane >= base) & (lane < base + FX_G), moved, slab)
        kbfx_ref[0, :, h * LANES:(h + 1) * LANES] = slab.astype(BF16)

    gates_ref[0] = _sigmoid(_dot(xb, wg_ref[...]))


def _proj_call(x, cos, sin, wqkv, wf, wg, bf, *, tm):
    B, T, D = x.shape
    nt = T // tm
    row = lambda b, i: (b, i, 0)
    tab = pl.BlockSpec((tm, LANES), lambda b, i: (i, 0))
    def rows(w, dt):
        return SDS((B, T, w), dt), pl.BlockSpec((1, tm, w), row)

    def transposed(w, n_heads):
        wt = w + n_heads * SUM_ROWS
        return SDS((B, nt, wt, tm), BF16), pl.BlockSpec((1, 1, wt, tm), lambda b, i: (b, i, 0, 0))

    outs = [rows(QDA_W, BF16), rows(KDA_W, F32), rows(KDA_W, BF16), rows(VDA_W, F32), transposed(VDA_W, DA_KV),
            rows(QFX_W, BF16), rows(KFX_W, F32), rows(VFX_W, F32), rows(FX_KV * LANES, BF16), transposed(VFX_W, FX_KV),
            rows(FX_HEADS, F32), rows(2 * D, F32)]
    out_shape = [o[0] for o in outs]
    out_specs = [o[1] for o in outs]
    return pl.pallas_call(
        functools.partial(_proj_kernel, tm=tm),
        grid=(B, nt),
        in_specs=[pl.BlockSpec((1, tm, D), row), tab, tab, _full(wqkv.shape), _full(wf.shape), _full(wg.shape),
                  _full(bf.shape)],
        out_specs=out_specs,
        out_shape=out_shape,
        scratch_shapes=[pltpu.VMEM((SUBLANES, LANES), F32)],
        compiler_params=_cparams(("arbitrary", "arbitrary")),
        name="proj",
    )(x, cos, sin, wqkv, wf, wg, bf)


def _rope_tables(pos):
    half = HEAD_DIM // 2
    inv = ROPE_THETA ** (-jnp.arange(0, HEAD_DIM, 2, dtype=F32) / HEAD_DIM)
    ang = pos.astype(F32)[:, None] * inv[None, :]
    cos, sin = jnp.cos(ang), jnp.sin(ang)
    reps = LANES // half
    cos_t = jnp.tile(cos, (1, reps))
    sin_t = jnp.concatenate([-sin, sin] * (reps // 2), axis=1)
    return cos_t, sin_t


def _attn_kernel(*refs, fox, tq, tk, lam_init):
    if fox:
        q_ref, k_ref, vt_ref, o_ref, qs_ref, sa_ref, sb_ref, m_ref, acc_ref = refs
    else:
        lam_ref, g_ref, q_ref, k_ref, vt_ref, o_ref, qs_ref, sa_ref, sb_ref, m_ref, acc_ref = refs
    qi = pl.program_id(2)
    ncol = 4 * tq
    hd = acc_ref.shape[0] - SUM_ROWS
    lane = lax.broadcasted_iota(I32, (tq, LANES), 1)
    lower = lane < HEAD_DIM

    if fox:
        qf = q_ref[0].astype(F32)
        for g in range(FX_G):
            slab = qf[:, (g // 2) * LANES:(g // 2 + 1) * LANES]
            if g % 2:
                slab = pltpu.roll(slab, HEAD_DIM, 1)
            ones = (lane == HEAD_DIM + g) | (lane == HEAD_DIM + FX_G + g) | (lane == HEAD_DIM + 2 * FX_G + g)
            qs_ref[g * tq:(g + 1) * tq, :] = jnp.where(lower, slab, jnp.where(ones, 1.0, 0.0)).astype(BF16)
    else:
        q = q_ref[0]
        zero = jnp.zeros_like(q[:, :LANES])
        for g in range(DA_G):
            slab = q[:, g * LANES:(g + 1) * LANES]
            qs_ref[(2 * g) * tq:(2 * g + 1) * tq, :] = jnp.where(lower, slab, zero)
            qs_ref[(2 * g + 1) * tq:(2 * g + 2) * tq, :] = jnp.where(lower, zero, slab)

    m_ref[...] = jnp.full_like(m_ref, -jnp.inf)
    acc_ref[...] = jnp.zeros_like(acc_ref)

    def scores(kb):
        start = pl.multiple_of(kb * tk, tk)
        return _dot_nt(k_ref[0, pl.ds(start, tk), :], qs_ref[...])

    def consume(s_ref, kb, masked):
        s = s_ref[...]
        if masked:
            key = lax.broadcasted_iota(I32, (tk, ncol), 0)
            qry = lax.broadcasted_iota(I32, (tk, ncol), 1) % tq
            s = jnp.where(key <= qry, s, -jnp.inf)
        m_prev = m_ref[...]
        m_new = jnp.maximum(m_prev, jnp.max(s, axis=0, keepdims=True))
        alpha = jnp.exp2(m_prev - m_new)
        p = jnp.exp2(s - m_new)
        pb = p.astype(BF16)
        tv = vt_ref.shape[-1]
        pv = _dot(vt_ref[0, kb * (tk // tv)], pb[0:tv])
        for j in range(1, tk // tv):
            pv = pv + _dot(vt_ref[0, kb * (tk // tv) + j], pb[j * tv:(j + 1) * tv])
        acc_ref[...] = alpha * acc_ref[...] + pv
        m_ref[...] = m_new

    sa_ref[...] = scores(0)

    def body(t, carry):
        kb = 2 * t
        sb_ref[...] = scores(kb + 1)
        consume(sa_ref, kb, False)
        sa_ref[...] = scores(kb + 2)
        consume(sb_ref, kb + 1, False)
        return carry

    lax.fori_loop(0, qi // 2, body, 0)

    @pl.when(qi % 2 == 0)
    def _():
        consume(sa_ref, qi, True)

    @pl.when(qi % 2 == 1)
    def _():
        sb_ref[...] = scores(qi)
        consume(sa_ref, qi - 1, False)
        consume(sb_ref, qi, True)

    o = acc_ref[0:hd, :] / acc_ref[hd:hd + 1, :]
    if fox:
        for c in range(FX_G // 2):
            pair = jnp.concatenate([o[:, (2 * c) * tq:(2 * c + 1) * tq], o[:, (2 * c + 1) * tq:(2 * c + 2) * tq]], axis=0)
            o_ref[0, :, c * LANES:(c + 1) * LANES] = pair.T.astype(o_ref.dtype)
    else:
        lam = lam_ref[0]
        for g in range(DA_G):
            og = (o[:, (2 * g) * tq:(2 * g + 1) * tq] - lam * o[:, (2 * g + 1) * tq:(2 * g + 2) * tq]).T
            og = og * lax.rsqrt(jnp.mean(og * og, axis=-1, keepdims=True) + EPS) * g_ref[...] * (1.0 - lam_init)
            o_ref[0, :, g * LANES:(g + 1) * LANES] = og.astype(o_ref.dtype)


def _attn_call(q, k, vt, lam, subln, *, fox, tq, tk, lam_init):
    B, T, _ = q.shape
    nv, tv = vt.shape[1], vt.shape[3]
    assert tq == tk and tk % tv == 0
    hd = vt.shape[2] // 4
    qspec = pl.BlockSpec((1, tq, 2 * LANES), lambda b, h, i: (b, i, h))
    kspec = pl.BlockSpec((1, T, LANES), lambda b, h, i: (b, 0, h))
    vtspec = pl.BlockSpec((1, nv, hd, tv), lambda b, h, i: (b, 0, h, 0))
    if fox:
        args = (q, k, vt)
        in_specs = [qspec, kspec, vtspec]
    else:
        args = (lam, subln, q, k, vt)
        in_specs = [pl.BlockSpec(memory_space=pltpu.SMEM), _full(subln.shape), qspec, kspec, vtspec]
    return pl.pallas_call(
        functools.partial(_attn_kernel, fox=fox, tq=tq, tk=tk, lam_init=lam_init),
        grid=(B, 4, T // tq),
        in_specs=in_specs,
        out_specs=qspec,
        out_shape=SDS(q.shape, BF16),
        scratch_shapes=[pltpu.VMEM((4 * tq, LANES), BF16), pltpu.VMEM((tk, 4 * tq), F32), pltpu.VMEM((tk, 4 * tq), F32),
                        pltpu.VMEM((1, 4 * tq), F32), pltpu.VMEM((hd, 4 * tq), F32)],
        compiler_params=_cparams(("arbitrary", "arbitrary", "arbitrary")),
        name="attn_fox" if fox else "attn_da",
    )(*args)


def _lane_cumsum(x):
    lane = lax.broadcasted_iota(I32, x.shape, 1)
    d = 1
    while d < x.shape[1]:
        x = x + jnp.where(lane >= d, pltpu.roll(x, d, 1), 0.0)
        d *= 2
    return x


def _sample_kernel(pt_ref, lam_ref, g_ref, qda_ref, qfx_ref, kdan_ref, vdan_ref, kfxn_ref, vfxn_ref, lfn_ref, *rest,
                   npg, nsteps, tq, lam_init):
    kda = rest[0 * npg:1 * npg]
    vda = rest[1 * npg:2 * npg]
    kfx = rest[2 * npg:3 * npg]
    vfx = rest[3 * npg:4 * npg]
    lfp = rest[4 * npg:5 * npg]
    (oa_ref, of_ref, qbda_ref, qbfx_ref, mda_ref, lda_ref, accda_ref, mfx_ref, lfx_ref, accfx_ref,
     carry_ref, pad_ref) = rest[5 * npg:]
    j = pl.program_id(1)
    nr = LANES
    lane8 = lax.broadcasted_iota(I32, (tq, LANES), 1)
    rows_per_head_da = nr // DA_KV

    @pl.when(j == 0)
    def _():
        qbda_ref[...] = jnp.zeros_like(qbda_ref)
        qbfx_ref[...] = jnp.zeros_like(qbfx_ref)
        for h in range(DA_KV):
            for g in range(DA_G):
                slab = qda_ref[:, (h * DA_G + g) * LANES:(h * DA_G + g + 1) * LANES]
                for m in range(2):
                    rb = (h * DA_G + g) * 2 + m
                    qbda_ref[rb * tq:(rb + 1) * tq, h * LANES:(h + 1) * LANES] = jnp.where(
                        (lane8 // HEAD_DIM) == m, slab, 0.0)
        for h in range(FX_KV):
            for g in range(FX_G):
                head = h * FX_G + g
                slab = qfx_ref[:, (head // 2) * LANES:(head // 2 + 1) * LANES]
                if (head % 2) != (h % 2):
                    slab = pltpu.roll(slab, HEAD_DIM, 1)
                qbfx_ref[head * tq:(head + 1) * tq, (h // 2) * LANES:(h // 2 + 1) * LANES] = jnp.where(
                    (lane8 // HEAD_DIM) == (h % 2), slab, 0.0)
        for r in (mda_ref, mfx_ref):
            r[...] = jnp.full_like(r, -jnp.inf)
        for r in (lda_ref, lfx_ref, accda_ref, accfx_ref, carry_ref):
            r[...] = jnp.zeros_like(r)

    qbda = qbda_ref[...].astype(BF16)
    qbfx = qbfx_ref[...].astype(BF16)

    def softmax_update(s, m_ref, l_ref):
        m_prev = m_ref[...]
        m_new = jnp.maximum(m_prev, jnp.max(s, axis=1, keepdims=True))
        alpha = jnp.exp2(m_prev - m_new)
        p = jnp.exp2(s - m_new)
        l_ref[...] = alpha * l_ref[...] + jnp.sum(p, axis=1, keepdims=True)
        m_ref[...] = m_new
        return alpha, p.astype(BF16)

    def fox_bias(s, negc):
        return jnp.concatenate([s[hd * tq:(hd + 1) * tq] + negc[hd:hd + 1, :] for hd in range(FX_HEADS)], axis=0)

    s_da = jnp.concatenate([_dot(qbda, kda[p][...].astype(BF16)) for p in range(npg)], axis=1)
    alpha, pr = softmax_update(s_da, mda_ref, lda_ref)
    for h in range(DA_KV):
        rs = slice(h * rows_per_head_da, (h + 1) * rows_per_head_da)
        pv = 0.0
        for p in range(npg):
            vh = vda[p][pl.ds(h, LANES, stride=DA_KV), :].astype(BF16)
            pv = pv + _dot(pr[rs, p * LANES:(p + 1) * LANES], vh)
        accda_ref[rs, :] = alpha[rs] * accda_ref[rs, :] + pv

    s_list = []
    for p in range(npg):
        cum = _lane_cumsum(lfp[p][...]) + carry_ref[...]
        carry_ref[...] = jnp.broadcast_to(cum[:, LANES - 1:LANES], carry_ref.shape)
        s_list.append(fox_bias(_dot(qbfx, kfx[p][...].astype(BF16)), -LOG2E * cum))
    alpha, pr = softmax_update(jnp.concatenate(s_list, axis=1), mfx_ref, lfx_ref)
    pv = 0.0
    for p in range(npg):
        pv = pv + _dot_nt(pr[:, p * LANES:(p + 1) * LANES], vfx[p][...].astype(BF16))
    accfx_ref[...] = alpha * accfx_ref[...] + pv

    @pl.when(j == nsteps - 1)
    def _():
        rowq = lax.broadcasted_iota(I32, (nr, LANES), 0) % tq
        col = lax.broadcasted_iota(I32, (nr, LANES), 1)
        valid = (col < tq) & (col <= rowq)

        def padded(x):
            w = x.shape[1]
            wp = max(w, LANES)
            pad_ref[:, :wp] = jnp.zeros((LANES, wp), F32)
            pad_ref[0:tq, :w] = x
            return pad_ref[:, :wp]

        s = jnp.where(valid, _dot_nt(qbda, padded(kdan_ref[...]).astype(BF16)), -jnp.inf)
        alpha, pr = softmax_update(s, mda_ref, lda_ref)
        vpad = padded(vdan_ref[...]).astype(BF16)
        for h in range(DA_KV):
            rs = slice(h * rows_per_head_da, (h + 1) * rows_per_head_da)
            accda_ref[rs, :] = alpha[rs] * accda_ref[rs, :] + _dot(pr[rs], vpad[:, h * LANES:(h + 1) * LANES])

        lft = padded(lfn_ref[...]).T
        cum = _lane_cumsum(lft[:FX_HEADS]) + carry_ref[...]
        s = fox_bias(_dot_nt(qbfx, padded(kfxn_ref[...]).astype(BF16)), -LOG2E * cum)
        s = jnp.where(valid, s, -jnp.inf)
        alpha, pr = softmax_update(s, mfx_ref, lfx_ref)
        accfx_ref[...] = alpha * accfx_ref[...] + _dot(pr, padded(vfxn_ref[...]).astype(BF16))

        lam = lam_ref[0]
        oda = accda_ref[...] / lda_ref[...]
        for h in range(DA_KV):
            for g in range(DA_G):
                rb = (h * DA_G + g) * 2
                og = oda[rb * tq:(rb + 1) * tq] - lam * oda[(rb + 1) * tq:(rb + 2) * tq]
                og = og * lax.rsqrt(jnp.mean(og * og, axis=-1, keepdims=True) + EPS) * g_ref[...] * (1.0 - lam_init)
                oa_ref[:, (h * DA_G + g) * LANES:(h * DA_G + g + 1) * LANES] = og
        ofx = accfx_ref[...] / lfx_ref[...]
        for sl in range(FX_HEADS // 2):
            pieces = []
            for head in (2 * sl, 2 * sl + 1):
                h = head // FX_G
                piece = ofx[head * tq:(head + 1) * tq, (h // 2) * LANES:(h // 2 + 1) * LANES]
                if (head % 2) != (h % 2):
                    piece = pltpu.roll(piece, HEAD_DIM, 1)
                pieces.append(piece)
            of_ref[:, sl * LANES:(sl + 1) * LANES] = jnp.where(lane8 < HEAD_DIM, pieces[0], pieces[1])


def _sample_call(page_table, lam, subln, qda, qfx, kdan, vdan, kfxn, vfxn, lfn, kda_c, vda_c, kfx_c, vfx_c, lf_c,
                 *, npg, lam_init):
    S, tq, _ = qda.shape
    n_pages = page_table.shape[1]
    nsteps = n_pages // npg
    assert nsteps * npg == n_pages and FX_HEADS * tq == LANES

    def new(w):
        return pl.BlockSpec((None, tq, w), lambda s, j, pt: (s, 0, 0))

    def page(rows, p):
        return pl.BlockSpec((None, rows, LANES), lambda s, j, pt: (pt[s, j * npg + p], 0, 0))

    in_specs = [pl.BlockSpec(memory_space=pltpu.SMEM), pl.BlockSpec(subln.shape, lambda s, j, pt: (0, 0)),
                new(QDA_W), new(QFX_W), new(KDA_W), new(VDA_W), new(KFX_W), new(VFX_W), new(FX_HEADS)]
    args = [lam, subln, qda, qfx, kdan, vdan, kfxn, vfxn, lfn]
    for arr in (kda_c, vda_c, kfx_c, vfx_c, lf_c):
        for p in range(npg):
            in_specs.append(page(arr.shape[1], p))
            args.append(arr)
    out_spec = pl.BlockSpec((None, tq, QDA_W), lambda s, j, pt: (s, 0, 0))
    grid_spec = pltpu.PrefetchScalarGridSpec(
        num_scalar_prefetch=1,
        grid=(S, nsteps),
        in_specs=in_specs,
        out_specs=[out_spec, out_spec],
        scratch_shapes=[pltpu.VMEM((LANES, KDA_W), F32), pltpu.VMEM((LANES, KFX_W), F32),
                        pltpu.VMEM((LANES, 1), F32), pltpu.VMEM((LANES, 1), F32), pltpu.VMEM((LANES, LANES), F32),
                        pltpu.VMEM((LANES, 1), F32), pltpu.VMEM((LANES, 1), F32), pltpu.VMEM((LANES, KFX_W), F32),
                        pltpu.VMEM((FX_HEADS, LANES), F32), pltpu.VMEM((LANES, KDA_W), F32)],
    )
    return pl.pallas_call(
        functools.partial(_sample_kernel, npg=npg, nsteps=nsteps, tq=tq, lam_init=lam_init),
        grid_spec=grid_spec,
        out_shape=[SDS((S, tq, QDA_W), F32), SDS((S, tq, QFX_W), F32)],
        compiler_params=_cparams(("arbitrary", "arbitrary")),
        name="sample_attn",
    )(page_table, *args)


def _mix_kernel(oa_ref, of_ref, gates_ref, x_ref, wa_ref, wb_ref, wo_ref, g_ref, b_ref, h_ref, *, alpha):
    d = x_ref.shape[-1]
    a = _dot(oa_ref[...].astype(BF16), wa_ref[...])
    b = _dot(of_ref[...].astype(BF16), wb_ref[...])
    merged = gates_ref[:, :d] * a + gates_ref[:, d:] * b
    mix = _dot(merged.astype(BF16), wo_ref[...])
    h_ref[...] = _layer_norm(alpha * x_ref[...] + mix, g_ref[...], b_ref[...])


def _mix_call(oa, of, gates, x, wa, wb, wo, g, b, *, alpha, tm):
    n, d = x.shape
    row = lambda i: (i, 0)
    return pl.pallas_call(
        functools.partial(_mix_kernel, alpha=alpha),
        grid=(n // tm,),
        in_specs=[pl.BlockSpec((tm, oa.shape[1]), row), pl.BlockSpec((tm, of.shape[1]), row),
                  pl.BlockSpec((tm, 2 * d), row), pl.BlockSpec((tm, d), row),
                  _full(wa.shape), _full(wb.shape), _full(wo.shape), _full(g.shape), _full(b.shape)],
        out_specs=pl.BlockSpec((tm, d), row),
        out_shape=SDS((n, d), F32),
        compiler_params=_cparams(("arbitrary",)),
        name="mix_ln1",
    )(oa, of, gates, x, wa, wb, wo, g, b)


def _route(scores, biased):
    n_exp, tm = scores.shape
    rid = lax.broadcasted_iota(I32, (n_exp, tm), 0)
    gr = lax.broadcasted_iota(I32, (GROUP_SIZE, tm), 0)
    neg = -jnp.inf
    gscore = []
    for g in range(N_GROUPS):
        v = biased[g * GROUP_SIZE:(g + 1) * GROUP_SIZE]
        m1 = jnp.max(v, axis=0, keepdims=True)
        i1 = jnp.min(jnp.where(v == m1, gr, GROUP_SIZE), axis=0, keepdims=True)
        m2 = jnp.max(jnp.where(gr == i1, neg, v), axis=0, keepdims=True)
        gscore.append(m1 + m2)
    chosen = [jnp.zeros((1, tm), I32) for _ in range(N_GROUPS)]
    for _ in range(TOPK_GROUPS):
        m = functools.reduce(jnp.maximum, gscore)
        first = functools.reduce(jnp.minimum, [jnp.where(gscore[g] == m, g, N_GROUPS) for g in range(N_GROUPS)])
        for g in range(N_GROUPS):
            hit = first == g
            chosen[g] = jnp.where(hit, 1, chosen[g])
            gscore[g] = jnp.where(hit, neg, gscore[g])
    cur = jnp.concatenate(
        [jnp.where(chosen[g] > 0, biased[g * GROUP_SIZE:(g + 1) * GROUP_SIZE], neg) for g in range(N_GROUPS)], axis=0)
    kid = lax.broadcasted_iota(I32, (TOP_K, tm), 0)
    idx = jnp.zeros((TOP_K, tm), I32)
    w = jnp.zeros((TOP_K, tm), F32)
    for k in range(TOP_K):
        m = jnp.max(cur, axis=0, keepdims=True)
        i = jnp.min(jnp.where(cur == m, rid, n_exp), axis=0, keepdims=True)
        hit = rid == i
        wk = jnp.sum(jnp.where(hit, scores, 0.0), axis=0, keepdims=True)
        idx = jnp.where(kid == k, i, idx)
        w = jnp.where(kid == k, wk, w)
        cur = jnp.where(hit, neg, cur)
    w = w / jnp.sum(w, axis=0, keepdims=True) * ROUTED_SCALE
    return idx, w


def _ffn_kernel(h_ref, p_ref, wrt_ref, rb_ref, wsg_ref, wsu_ref, wsd_ref, wpp_ref, wpg_ref,
                base_ref, idx_ref, w_ref, hrows_ref, *, alpha):
    h = h_ref[...]
    hrows_ref[...] = h.reshape(hrows_ref.shape)
    hb = h.astype(BF16)
    scores = _sigmoid(_dot_nt(wrt_ref[...], hb))
    idx, w = _route(scores, scores + rb_ref[...])
    idx_ref[...] = idx
    w_ref[...] = w
    gate = _dot(hb, wsg_ref[...])
    shared = _dot((gate * _sigmoid(gate) * _dot(hb, wsu_ref[...])).astype(BF16), wsd_ref[...])
    ple = _sigmoid(_dot(hb, wpg_ref[...])) * _dot(p_ref[...].astype(BF16), wpp_ref[...])
    base_ref[...] = alpha * h + shared + ple


def _ffn_call(h, p, wrt, rbias, wsg, wsu, wsd, wpp, wpg, *, alpha, tm):
    n, d = h.shape
    row = lambda i: (i, 0)
    col = lambda i: (0, i)
    return pl.pallas_call(
        functools.partial(_ffn_kernel, alpha=alpha),
        grid=(n // tm,),
        in_specs=[pl.BlockSpec((tm, d), row), pl.BlockSpec((tm, p.shape[1]), row), _full(wrt.shape), _full(rbias.shape),
                  _full(wsg.shape), _full(wsu.shape), _full(wsd.shape), _full(wpp.shape), _full(wpg.shape)],
        out_specs=[pl.BlockSpec((tm, d), row), pl.BlockSpec((TOP_K, tm), col), pl.BlockSpec((TOP_K, tm), col),
                   pl.BlockSpec((tm, 1, d), lambda i: (i, 0, 0))],
        out_shape=[SDS((n, d), F32), SDS((TOP_K, n), I32), SDS((TOP_K, n), F32), SDS((n, 1, d), F32)],
        compiler_params=_cparams(("arbitrary",)),
        name="router_shared_ple",
    )(h, p, wrt, rbias, wsg, wsu, wsd, wpp, wpg)


def _gather_rows(idx_ref, n, src_hbm, dst, sem):
    def body(r, carry):
        pltpu.make_async_copy(src_hbm.at[idx_ref[0, r]], dst.at[r], sem).start()
        return carry
    lax.fori_loop(0, n, body, 0, unroll=8)


def _wait_rows(n, src_hbm, dst, sem):
    pltpu.make_async_copy(src_hbm.at[pl.ds(0, n)], dst, sem).wait()


def _moe_kernel(be_ref, on_ref, tokc_ref, tokn_ref, sw_ref, h_hbm, wg_ref, wu_ref, wd_ref, o_ref,
                xbuf, x2d, wgb, wub, wdb, sem, *, rb, nb):
    i = pl.program_id(0)
    slot = i % 2
    nxt = jnp.minimum(i + 1, nb - 1)

    @pl.when((i == 0) & (on_ref[0] > 0))
    def _():
        _gather_rows(tokc_ref, rb, h_hbm, xbuf.at[0], sem.at[0])

    @pl.when((i + 1 < nb) & (on_ref[nxt] > 0))
    def _():
        _gather_rows(tokn_ref, rb, h_hbm, xbuf.at[1 - slot], sem.at[1 - slot])

    @pl.when(on_ref[i] > 0)
    def _():
        @pl.when((i == 0) | (be_ref[i] != be_ref[jnp.maximum(i - 1, 0)]))
        def _():
            wgb[...] = wg_ref[...].astype(BF16)
            wub[...] = wu_ref[...].astype(BF16)
            wdb[...] = wd_ref[...].astype(BF16)

        _wait_rows(rb, h_hbm, xbuf.at[slot], sem.at[slot])
        x2d[...] = xbuf[slot].reshape(x2d.shape)
        x = x2d[...].astype(BF16)
        gate = _dot(x, wgb[...])
        hid = gate * _sigmoid(gate) * _dot(x, wub[...])
        y = _dot(hid.astype(BF16), wdb[...]) * sw_ref[...]
        o_ref[...] = y.reshape(o_ref.shape)

    @pl.when(on_ref[i] == 0)
    def _():
        o_ref[...] = jnp.zeros_like(o_ref)


def _moe_call(block_e, block_on, slot_tok, slot_w, h_rows, wg, wu, wd, *, rb):
    nb = block_e.shape[0]
    d = h_rows.shape[-1]
    de = wg.shape[-1]
    tok3 = slot_tok.reshape(nb, 1, rb)
    sw3 = slot_w.reshape(nb, rb, 1)
    grid_spec = pltpu.PrefetchScalarGridSpec(
        num_scalar_prefetch=2,
        grid=(nb,),
        in_specs=[pl.BlockSpec((None, 1, rb), lambda i, be, on: (i, 0, 0), memory_space=pltpu.SMEM),
                  pl.BlockSpec((None, 1, rb), lambda i, be, on: (jnp.minimum(i + 1, nb - 1), 0, 0),
                               memory_space=pltpu.SMEM),
                  pl.BlockSpec((None, rb, 1), lambda i, be, on: (i, 0, 0)),
                  pl.BlockSpec(memory_space=pl.ANY),
                  pl.BlockSpec((None, d, de), lambda i, be, on: (be[i], 0, 0)),
                  pl.BlockSpec((None, d, de), lambda i, be, on: (be[i], 0, 0)),
                  pl.BlockSpec((None, de, d), lambda i, be, on: (be[i], 0, 0))],
        out_specs=pl.BlockSpec((rb, 1, d), lambda i, be, on: (i, 0, 0)),
        scratch_shapes=[pltpu.VMEM((2, rb, 1, d), F32), pltpu.VMEM((rb, d), F32), pltpu.VMEM((d, de), BF16),
                        pltpu.VMEM((d, de), BF16),
                        pltpu.VMEM((de, d), BF16), pltpu.SemaphoreType.DMA((2,))],
    )
    return pl.pallas_call(
        functools.partial(_moe_kernel, rb=rb, nb=nb),
        grid_spec=grid_spec,
        out_shape=SDS((nb * rb, 1, d), F32),
        compiler_params=_cparams(("arbitrary",), disable_bounds_checks=True),
        name="moe_experts",
    )(block_e, block_on, tok3, tok3, sw3, h_rows, wg, wu, wd)


def _moe_metadata(idx, w, n, rb):
    a = TOP_K * n
    assert a % rb == 0
    n_dummy = N_EXPERTS * rb
    flat_e = idx.reshape(-1)
    experts = jnp.arange(N_EXPERTS, dtype=I32)
    counts = jnp.sum((flat_e[:, None] == experts[None, :]).astype(I32), axis=0)
    n_fill = (-counts) % rb
    fill_key = jnp.where(jnp.arange(rb, dtype=I32)[None, :] < n_fill[:, None], 2 * experts[:, None] + 1, 2 * N_EXPERTS)
    keys = jnp.concatenate([2 * flat_e, fill_key.reshape(-1)])
    assign = jnp.concatenate([jnp.arange(a, dtype=I32), jnp.full((n_dummy,), a, I32)])
    weight = jnp.concatenate([w.reshape(-1), jnp.zeros((n_dummy,), F32)])
    keys_s, assign_s, slot_w = lax.sort((keys, assign, weight), num_keys=1)
    slot_tok = jnp.where(assign_s < a, assign_s % n, 0)
    block_key = keys_s[::rb]
    block_e = jnp.minimum(block_key // 2, N_EXPERTS - 1)
    block_on = (block_key < 2 * N_EXPERTS).astype(I32)
    _, slot_of = lax.sort((assign_s, jnp.arange(a + n_dummy, dtype=I32)), num_keys=1)
    return block_e, block_on, slot_tok, slot_w, slot_of[:a].reshape(TOP_K, n)


def _combine_kernel(posc_ref, posn_ref, y_hbm, base_ref, g_ref, b_ref, o_ref, buf, acc2d, sem, *, tt, nt):
    i = pl.program_id(0)
    slot = i % 2
    nrows = TOP_K * tt

    @pl.when(i == 0)
    def _():
        _gather_rows(posc_ref, nrows, y_hbm, buf.at[0], sem.at[0])

    @pl.when(i + 1 < nt)
    def _():
        _gather_rows(posn_ref, nrows, y_hbm, buf.at[1 - slot], sem.at[1 - slot])

    _wait_rows(nrows, y_hbm, buf.at[slot], sem.at[slot])
    tot = buf[slot, 0:tt]
    for k in range(1, TOP_K):
        tot = tot + buf[slot, k * tt:(k + 1) * tt]
    acc2d[...] = tot.reshape(tt, acc2d.shape[-1])
    o_ref[...] = _layer_norm(base_ref[...] + acc2d[...], g_ref[...], b_ref[...])


def _combine_call(pos, y_rows, base, g, b, *, tt):
    n, d = base.shape
    nt = n // tt
    pos3 = pos.reshape(TOP_K, nt, tt).transpose(1, 0, 2).reshape(nt, 1, TOP_K * tt)
    row = lambda i: (i, 0)
    return pl.pallas_call(
        functools.partial(_combine_kernel, tt=tt, nt=nt),
        grid=(nt,),
        in_specs=[pl.BlockSpec((None, 1, TOP_K * tt), lambda i: (i, 0, 0), memory_space=pltpu.SMEM),
                  pl.BlockSpec((None, 1, TOP_K * tt), lambda i: (jnp.minimum(i + 1, nt - 1), 0, 0),
                               memory_space=pltpu.SMEM),
                  pl.BlockSpec(memory_space=pl.ANY),
                  pl.BlockSpec((tt, d), row), _full(g.shape), _full(b.shape)],
        out_specs=pl.BlockSpec((tt, d), row),
        out_shape=SDS((n, d), F32),
        scratch_shapes=[pltpu.VMEM((2, TOP_K * tt, 1, d), F32), pltpu.VMEM((tt, d), F32), pltpu.SemaphoreType.DMA((2,))],
        compiler_params=_cparams(("arbitrary",), disable_bounds_checks=True),
        name="combine_ln2",
    )(pos3, pos3, y_rows, base, g, b)


def kernel(x_prompt, x_sample, cache_da_k, cache_da_v, cache_fox_k, cache_fox_v, cache_fox_logf, page_table, p_prompt, p_sample, w_in, b_forget, lambda_q1, lambda_k1, lambda_q2, lambda_k2, subln_g, w_branch_a, w_branch_b, w_out, ln1_g, ln1_b, w_router, router_bias, w_exp_gate, w_exp_up, w_exp_down, w_sh_gate, w_sh_up, w_sh_down, w_ple_proj, w_ple_gate, ln2_g, ln2_b):
    depth = w_in.shape[0]
    alpha = (2 * depth) ** 0.25
    B, T, D = x_prompt.shape
    S, TQ, _ = x_sample.shape
    n_pool, page_size = cache_da_k.shape[1], cache_da_k.shape[2]
    past_len = page_table.shape[1] * page_size
    n_p, n_s = B * T, S * TQ
    n = n_p + n_s
    cos_p, sin_p = _rope_tables(jnp.arange(T))
    cos_s, sin_s = _rope_tables(jnp.tile(past_len + jnp.arange(TQ), S))
    xp, xs = x_prompt, x_sample
    outs_p, outs_s = [[] for _ in range(5)], [[] for _ in range(5)]
    for l in range(depth):
        lam_init = 0.8 - 0.6 * math.exp(-0.3 * l)
        lam = (jnp.exp(jnp.sum(lambda_q1[l] * lambda_k1[l])) - jnp.exp(jnp.sum(lambda_q2[l] * lambda_k2[l]))
               + lam_init).reshape(1).astype(F32)
        subln = subln_g[l].reshape(1, -1)
        wl = w_in[l]
        wqkv = wl[:, :QKV_W].astype(BF16)
        wf = jnp.pad(wl[:, QKV_W:QKV_W + FX_HEADS], ((0, 0), (0, LANES - FX_HEADS))).astype(BF16)
        wg = wl[:, QKV_W + FX_HEADS:].astype(BF16)
        bf = jnp.pad(b_forget[l], (0, LANES - FX_HEADS)).reshape(1, LANES)

        (qda, kda, kdab, vda, vdat, qfx, kfx, vfx, kbfx, vfxt, logf, gates) = _proj_call(
            xp, cos_p, sin_p, wqkv, wf, wg, bf, tm=TOKEN_TILE)
        oa_p = _attn_call(qda, kdab, vdat, lam, subln, fox=False, tq=Q_TILE, tk=Q_TILE, lam_init=lam_init)
        of_p = _attn_call(qfx, kbfx, vfxt, None, None, fox=True, tq=Q_TILE, tk=Q_TILE, lam_init=lam_init)
        for dst, arr in zip(outs_p, (kda.reshape(B, T, DA_KV, 2, HEAD_DIM), vda.reshape(B, T, DA_KV, 2 * HEAD_DIM),
                                     kfx.reshape(B, T, FX_KV, HEAD_DIM), vfx.reshape(B, T, FX_KV, HEAD_DIM), logf)):
            dst.append(arr)
        gates_p = gates.reshape(n_p, 2 * D)

        (qda, kda, _, vda, _, qfx, kfx, vfx, _, _, logf, gates) = _proj_call(
            xs.reshape(1, n_s, D), cos_s, sin_s, wqkv, wf, wg, bf, tm=TOKEN_TILE)
        per_seq = lambda a: a.reshape(S, TQ, a.shape[-1])
        kT_da = jnp.transpose(cache_da_k[l], (0, 2, 3, 4, 1)).reshape(n_pool, KDA_W, page_size)
        v_da = cache_da_v[l].reshape(n_pool, page_size * DA_KV, 2 * HEAD_DIM)
        kT_fx = jnp.transpose(cache_fox_k[l], (0, 2, 3, 1)).reshape(n_pool, KFX_W, page_size)
        vT_fx = jnp.transpose(cache_fox_v[l], (0, 2, 3, 1)).reshape(n_pool, VFX_W, page_size)
        lfT = jnp.transpose(cache_fox_logf[l], (0, 2, 1))
        oa_s, of_s = _sample_call(page_table, lam, subln, per_seq(qda).astype(F32), per_seq(qfx).astype(F32),
                                  per_seq(kda), per_seq(vda), per_seq(kfx), per_seq(vfx), per_seq(logf),
                                  kT_da, v_da, kT_fx, vT_fx, lfT, npg=PAGES_PER_STEP, lam_init=lam_init)
        for dst, arr in zip(outs_s, (kda.reshape(S, TQ, DA_KV, 2, HEAD_DIM), vda.reshape(S, TQ, DA_KV, 2 * HEAD_DIM),
                                     kfx.reshape(S, TQ, FX_KV, HEAD_DIM), vfx.reshape(S, TQ, FX_KV, HEAD_DIM),
                                     logf.reshape(S, TQ, FX_HEADS))):
            dst.append(arr)
        gates_s = gates.reshape(n_s, 2 * D)

        wa, wb, wo = (w.astype(BF16) for w in (w_branch_a[l], w_branch_b[l], w_out[l]))
        g1, b1 = ln1_g[l].reshape(1, D), ln1_b[l].reshape(1, D)
        h_p = _mix_call(oa_p.reshape(n_p, -1), of_p.reshape(n_p, -1), gates_p, xp.reshape(n_p, D), wa, wb, wo, g1, b1,
                        alpha=alpha, tm=TOKEN_TILE)
        h_s = _mix_call(oa_s.reshape(n_s, -1), of_s.reshape(n_s, -1), gates_s, xs.reshape(n_s, D), wa, wb, wo, g1, b1,
                        alpha=alpha, tm=TOKEN_TILE)
        h = jnp.concatenate([h_p, h_s], axis=0)
        p_all = jnp.concatenate([p_prompt[l].reshape(n_p, -1), p_sample[l].reshape(n_s, -1)], axis=0)
        base, idx, w, h_rows = _ffn_call(
            h, p_all, w_router[l].T.astype(BF16), router_bias[l].reshape(-1, 1), w_sh_gate[l].astype(BF16),
            w_sh_up[l].astype(BF16), w_sh_down[l].astype(BF16), w_ple_proj[l].astype(BF16), w_ple_gate[l].astype(BF16),
            alpha=alpha, tm=TOKEN_TILE)
        block_e, block_on, slot_tok, slot_w, slot_of = _moe_metadata(idx, w, n, EXPERT_ROWS)
        y_rows = _moe_call(block_e, block_on, slot_tok, slot_w, h_rows, w_exp_gate[l], w_exp_up[l], w_exp_down[l],
                           rb=EXPERT_ROWS)
        y = _combine_call(slot_of, y_rows, base, ln2_g[l].reshape(1, D), ln2_b[l].reshape(1, D), tt=COMBINE_TILE)
        xp, xs = y[:n_p].reshape(B, T, D), y[n_p:].reshape(S, TQ, D)
    stack = lambda xs_: jnp.stack(xs_)
    return (xp, xs, *(stack(o) for o in outs_p), *(stack(o) for o in outs_s))
```
